```python
import math
import jax
import jax.numpy as jnp
from jax import lax
import numpy as np

D_MODEL = 1024
BATCH = 32
SEQ = 2048
DEPTH = 1

SSM_EXPAND = 2
SSM_D_INNER = SSM_EXPAND * D_MODEL
SSM_HEAD_DIM = 64
SSM_HEADS = SSM_D_INNER // SSM_HEAD_DIM
SSM_GROUPS = 4
SSM_HPG = SSM_HEADS // SSM_GROUPS
SSM_STATE = 128
SSM_CONV = 4
SSM_CHUNK = 128
SSM_CONV_DIM = SSM_D_INNER + 2 * SSM_GROUPS * SSM_STATE

NSA_HEADS = 16
NSA_HEAD_DIM = 64
NSA_WIDTH = NSA_HEADS * NSA_HEAD_DIM
NSA_KV_GROUPS = 2
NSA_HPG = NSA_HEADS // NSA_KV_GROUPS
NSA_KV_WIDTH = NSA_KV_GROUPS * NSA_HEAD_DIM
CMP_BLOCK = 32
CMP_STRIDE = 16
CMP_HIDDEN = 256
SEL_BLOCK = 64
SEL_TOPK = 4
WINDOW = 512
NSA_QBLOCK = 64
ROPE_THETA = 10000.0

MOE_GROUPS = 4
EXPERTS_PER_GROUP = 8
N_EXPERTS = MOE_GROUPS * EXPERTS_PER_GROUP
EXPERT_TOPK = 2
EXPERT_HIDDEN = 256

NORM_EPS = 1e-6

IN_SIZES = (SSM_D_INNER, SSM_CONV_DIM, SSM_HEADS, NSA_WIDTH, NSA_KV_WIDTH, NSA_KV_WIDTH, NSA_KV_WIDTH, NSA_KV_WIDTH, NSA_KV_WIDTH, NSA_KV_WIDTH, 3 * NSA_HEADS, D_MODEL, D_MODEL)
IN_WIDTH = SSM_D_INNER + SSM_CONV_DIM + SSM_HEADS + NSA_WIDTH + 6 * NSA_KV_WIDTH + 3 * NSA_HEADS + 2 * D_MODEL

kernel_name = 'hybrid_ssd_nsa_hmoe_adaln_block'


def _rmsnorm(x, g):
    xf = x.astype(jnp.float32)
    y = xf * lax.rsqrt(jnp.mean(xf * xf, axis=-1, keepdims=True) + NORM_EPS)
    return (y * g.astype(jnp.float32)).astype(x.dtype)


def _masked_softmax(s, mask):
    s = jnp.where(mask, s.astype(jnp.float32), -1e30)
    m = jnp.max(s, axis=-1, keepdims=True)
    p = jnp.where(mask, jnp.exp(s - m), 0.0)
    return p / jnp.maximum(jnp.sum(p, axis=-1, keepdims=True), 1e-30)


def _rope(t, cos, sin):
    shape = cos.shape[:2] + (1,) * (t.ndim - 3) + cos.shape[2:]
    cos = cos.reshape(shape)
    sin = sin.reshape(shape)
    t1, t2 = jnp.split(t.astype(jnp.float32), 2, axis=-1)
    return jnp.concatenate([t1 * cos - t2 * sin, t2 * cos + t1 * sin], axis=-1).astype(t.dtype)


def _ssd_chunked_scan(xh, dt, a, bm, cm):
    b, s = xh.shape[:2]
    nc = s // SSM_CHUNK

    def chunks(t):
        t = t.astype(jnp.float32)
        return jnp.moveaxis(t.reshape((b, nc, SSM_CHUNK) + t.shape[2:]), 1, 0)

    tril = jnp.tril(jnp.ones((SSM_CHUNK, SSM_CHUNK), dtype=bool))[None, :, :, None, None]
    a = a.astype(jnp.float32)

    def step(state, inp):
        xc, dtc, bc, cc = inp
        acum = jnp.cumsum(dtc * a, axis=1)
        seg = acum[:, :, None] - acum[:, None, :]
        decay = jnp.exp(jnp.where(tril, seg, -jnp.inf))
        cb = jnp.einsum('btgn,bsgn->bgts', cc, bc)
        y_diag = jnp.einsum('bgts,btsgr,bsgr,bsgrp->btgrp', cb, decay, dtc, xc)
        y_off = jnp.einsum('btgn,bgrpn->btgrp', cc, state) * jnp.exp(acum)[..., None]
        w_end = jnp.exp(acum[:, -1:] - acum) * dtc
        new_state = state * jnp.exp(acum[:, -1])[..., None, None] + jnp.einsum('bsgn,bsgr,bsgrp->bgrpn', bc, w_end, xc)
        return new_state, y_diag + y_off

    state0 = jnp.zeros((b, SSM_GROUPS, SSM_HPG, SSM_HEAD_DIM, SSM_STATE), jnp.float32)
    _, ys = lax.scan(step, state0, (chunks(xh), chunks(dt), chunks(bm), chunks(cm)))
    return jnp.moveaxis(ys, 0, 1).reshape(xh.shape).astype(xh.dtype)


def _mamba2_mixer(h_z, h_xbc, h_dt, conv_w, conv_b, dt_bias, a_log, d_skip, ssm_norm_g):
    b, s, _ = h_xbc.shape
    xbc = lax.conv_general_dilated(h_xbc, conv_w[:, None, :], window_strides=(1,), padding=[(SSM_CONV - 1, 0)], dimension_numbers=('NWC', 'WIO', 'NWC'), feature_group_count=SSM_CONV_DIM)
    xbc = jax.nn.silu(xbc + conv_b)
    xs, bm, cm = jnp.split(xbc, [SSM_D_INNER, SSM_D_INNER + SSM_GROUPS * SSM_STATE], axis=-1)
    xh = xs.reshape(b, s, SSM_GROUPS, SSM_HPG, SSM_HEAD_DIM)
    bm = bm.reshape(b, s, SSM_GROUPS, SSM_STATE)
    cm = cm.reshape(b, s, SSM_GROUPS, SSM_STATE)
    dt = jax.nn.softplus((h_dt + dt_bias).astype(jnp.float32)).reshape(b, s, SSM_GROUPS, SSM_HPG)
    a = -jnp.exp(a_log.astype(jnp.float32)).reshape(SSM_GROUPS, SSM_HPG)
    y = _ssd_chunked_scan(xh, dt, a, bm, cm) + d_skip.reshape(SSM_GROUPS, SSM_HPG, 1) * xh
    y = y.reshape(b, s, SSM_D_INNER) * jax.nn.silu(h_z)
    yg = y.astype(jnp.float32).reshape(b, s, SSM_GROUPS, SSM_D_INNER // SSM_GROUPS)
    yg = yg * lax.rsqrt(jnp.mean(yg * yg, axis=-1, keepdims=True) + NORM_EPS)
    return (yg.reshape(b, s, SSM_D_INNER) * ssm_norm_g).astype(h_z.dtype)


def _compress(t, pe, w1, w2, idx):
    b = t.shape[0]
    blk = t[:, idx] + pe[None, None, :, None, :]
    blk = jnp.moveaxis(blk, 3, 2).reshape(b, idx.shape[0], NSA_KV_GROUPS, CMP_BLOCK * NSA_HEAD_DIM)
    return jax.nn.silu(blk @ w1) @ w2


def _nsa_mixer(q, kc, vc, ksl, vsl, kw, vw, gates, cos, sin, cmp_pe_k, cmp_w1_k, cmp_w2_k, cmp_pe_v, cmp_w1_v, cmp_w2_v):
    b, s = q.shape[:2]
    G, R, dk = NSA_KV_GROUPS, NSA_HPG, NSA_HEAD_DIM
    q = _rope(q.reshape(b, s, G, R, dk), cos, sin)
    kc = _rope(kc.reshape(b, s, G, dk), cos, sin)
    ksl = _rope(ksl.reshape(b, s, G, dk), cos, sin)
    kw = _rope(kw.reshape(b, s, G, dk), cos, sin)
    vc = vc.reshape(b, s, G, dk)
    vsl = vsl.reshape(b, s, G, dk)
    vw = vw.reshape(b, s, G, dk)
    gates = jax.nn.sigmoid(gates.astype(jnp.float32)).reshape(b, s, G, R, 3)
    n_cmp = (s - CMP_BLOCK) // CMP_STRIDE + 1
    cidx = jnp.arange(n_cmp)[:, None] * CMP_STRIDE + jnp.arange(CMP_BLOCK)[None, :]
    kcmp = _compress(kc, cmp_pe_k, cmp_w1_k, cmp_w2_k, cidx)
    vcmp = _compress(vc, cmp_pe_v, cmp_w1_v, cmp_w2_v, cidx)
    cstart = jnp.arange(n_cmp) * CMP_STRIDE
    cmp_end = cstart + CMP_BLOCK - 1
    n_sel = s // SEL_BLOCK
    sstart = jnp.arange(n_sel) * SEL_BLOCK
    overlap = ((cstart[:, None] < sstart[None, :] + SEL_BLOCK) & (cstart[:, None] + CMP_BLOCK > sstart[None, :])).astype(jnp.float32)
    ksb = jnp.moveaxis(ksl.reshape(b, n_sel, SEL_BLOCK, G, dk), 3, 1)
    vsb = jnp.moveaxis(vsl.reshape(b, n_sel, SEL_BLOCK, G, dk), 3, 1)
    kwp = jnp.pad(kw, ((0, 0), (WINDOW, 0), (0, 0), (0, 0)))
    vwp = jnp.pad(vw, ((0, 0), (WINDOW, 0), (0, 0), (0, 0)))
    scale = NSA_HEAD_DIM ** -0.5
    bi = jnp.arange(b)[:, None, None, None]
    gi = jnp.arange(G)[None, :, None, None]
    sel_j = jnp.arange(n_sel)
    win_off = jnp.arange(WINDOW + NSA_QBLOCK) - WINDOW

    def block(qi):
        s0 = qi * NSA_QBLOCK
        t = s0 + jnp.arange(NSA_QBLOCK)
        qb = lax.dynamic_slice_in_dim(q, s0, NSA_QBLOCK, axis=1)
        gb = lax.dynamic_slice_in_dim(gates, s0, NSA_QBLOCK, axis=1)
        mask_c = (cmp_end[None, :] <= t[:, None])[None, :, None, None, :]
        pc = _masked_softmax(jnp.einsum('bqgrd,bcgd->bqgrc', qb, kcmp) * scale, mask_c)
        o_cmp = jnp.einsum('bqgrc,bcgd->bqgrd', pc, vcmp)
        local = t // SEL_BLOCK
        prio = jnp.where(sel_j[None, :] == local[:, None], 1e6, jnp.where(sel_j[None, :] > local[:, None], -jnp.inf, jnp.where(sel_j[None, :] == 0, 5e5, 0.0)))
        imp = jnp.einsum('bqgrc,cj->bqgj', pc, overlap) + prio[None, :, None, :]
        _, sel = lax.top_k(imp, SEL_TOPK)
        sel = jnp.swapaxes(sel, 1, 2)
        k_g = ksb[bi, gi, sel]
        v_g = vsb[bi, gi, sel]
        kpos = sel[..., None] * SEL_BLOCK + jnp.arange(SEL_BLOCK)
        mask_s = jnp.swapaxes(kpos <= t[None, None, :, None, None], 1, 2).reshape(b, NSA_QBLOCK, G, 1, SEL_TOPK * SEL_BLOCK)
        ss = jnp.einsum('bqgrd,bgqkld->bqgrkl', qb, k_g).reshape(b, NSA_QBLOCK, G, R, SEL_TOPK * SEL_BLOCK) * scale
        ps = _masked_softmax(ss, mask_s).reshape(b, NSA_QBLOCK, G, R, SEL_TOPK, SEL_BLOCK)
        o_slc = jnp.einsum('bqgrkl,bgqkld->bqgrd', ps, v_g)
        kwb = lax.dynamic_slice_in_dim(kwp, s0, WINDOW + NSA_QBLOCK, axis=1)
        vwb = lax.dynamic_slice_in_dim(vwp, s0, WINDOW + NSA_QBLOCK, axis=1)
        kpos_w = s0 + win_off
        mask_w = ((kpos_w[None, :] <= t[:, None]) & (kpos_w[None, :] > t[:, None] - WINDOW) & (kpos_w[None, :] >= 0))[None, :, None, None, :]
        pw = _masked_softmax(jnp.einsum('bqgrd,bkgd->bqgrk', qb, kwb) * scale, mask_w)
        o_win = jnp.einsum('bqgrk,bkgd->bqgrd', pw, vwb)
        return gb[..., 0:1] * o_cmp + gb[..., 1:2] * o_slc + gb[..., 2:3] * o_win

    out = lax.map(block, jnp.arange(s // NSA_QBLOCK))
    return jnp.moveaxis(out, 0, 1).reshape(b, s, NSA_WIDTH).astype(q.dtype)


def _hier_moe(h, w_rg, b_rg, w_re, b_re, w_g, w_u, w_d):
    b, s, d = h.shape
    t = h.reshape(b * s, d)
    tf = t.astype(jnp.float32)
    p_group = jax.nn.softmax(tf @ w_rg.astype(jnp.float32) + b_rg.astype(jnp.float32), axis=-1)
    pg, gsel = lax.top_k(p_group, 1)
    le = (tf @ w_re.astype(jnp.float32) + b_re.astype(jnp.float32)).reshape(-1, MOE_GROUPS, EXPERTS_PER_GROUP)
    le_g = jnp.einsum('tge,tg->te', le, jax.nn.one_hot(gsel[:, 0], MOE_GROUPS, dtype=jnp.float32))
    pe = jax.nn.softmax(le_g, axis=-1)
    pv, pi = lax.top_k(pe, EXPERT_TOPK)
    wts = pg * pv / jnp.sum(pv, axis=-1, keepdims=True)
    ids = gsel * EXPERTS_PER_GROUP + pi
    combine = jnp.sum(jax.nn.one_hot(ids, N_EXPERTS, dtype=jnp.float32) * wts[..., None], axis=1)
    out = jnp.zeros((b * s, d), jnp.float32)
    for e in range(N_EXPERTS):
        he = jax.nn.silu(t @ w_g[e]) * (t @ w_u[e])
        out = out + combine[:, e:e + 1] * (he @ w_d[e])
    return out.reshape(b, s, d).astype(h.dtype)


def _hybrid_layer(h, c, cos, sin, w_ada, b_ada, norm1_g, w_in, conv_w, conv_b, dt_bias, a_log, d_skip, ssm_norm_g, w_ssm_out, cmp_pe_k, cmp_w1_k, cmp_w2_k, cmp_pe_v, cmp_w1_v, cmp_w2_v, w_nsa_out, w_o, norm2_g, w_router_group, b_router_group, w_router_expert, b_router_expert, w_exp_gate, w_exp_up, w_exp_down):
    ada = jax.nn.silu(c) @ w_ada + b_ada
    shift1, scale1, gate1, shift2, scale2, gate2 = [a[:, None, :] for a in jnp.split(ada, 6, axis=-1)]
    hn = _rmsnorm(h, norm1_g) * (1.0 + scale1) + shift1
    offsets = [int(o) for o in np.cumsum(IN_SIZES)[:-1]]
    w_z, w_xbc, w_dt, w_q, w_kc, w_vc, w_ks, w_vs, w_kw, w_vw, w_gn, w_gs, w_ga = jnp.split(w_in, offsets, axis=1)
    y_ssm = _mamba2_mixer(hn @ w_z, hn @ w_xbc, hn @ w_dt, conv_w, conv_b, dt_bias, a_log, d_skip, ssm_norm_g) @ w_ssm_out
    y_nsa = _nsa_mixer(hn @ w_q, hn @ w_kc, hn @ w_vc, hn @ w_ks, hn @ w_vs, hn @ w_kw, hn @ w_vw, hn @ w_gn, cos, sin, cmp_pe_k, cmp_w1_k, cmp_w2_k, cmp_pe_v, cmp_w1_v, cmp_w2_v) @ w_nsa_out
    merged = jax.nn.sigmoid(hn @ w_gs) * y_ssm + jax.nn.sigmoid(hn @ w_ga) * y_nsa
    h = h + gate1 * (merged @ w_o)
    hn2 = _rmsnorm(h, norm2_g) * (1.0 + scale2) + shift2
    h = h + gate2 * _hier_moe(hn2, w_router_group, b_router_group, w_router_expert, b_router_expert, w_exp_gate, w_exp_up, w_exp_down)
    return h


def setup_inputs(seed: int = 0) -> dict:
    key = jax.random.key(seed)
    ks = jax.random.split(key, 40)
    f32 = jnp.float32

    def nrm(k, shape, fan_in):
        return jax.random.normal(k, shape, f32) * fan_in ** -0.5

    def gain(k, shape):
        return 1.0 + 0.05 * jax.random.normal(k, shape, f32)

    def small(k, shape, s=0.02):
        return s * jax.random.normal(k, shape, f32)

    L = DEPTH
    dt0 = jnp.exp(jax.random.uniform(ks[9], (L, SSM_HEADS), f32) * (math.log(0.1) - math.log(0.001)) + math.log(0.001))
    return {
        'x': jax.random.normal(ks[0], (BATCH, SEQ, D_MODEL), f32),
        'c': jax.random.normal(ks[1], (BATCH, D_MODEL), f32),
        'positions': jnp.arange(SEQ, dtype=jnp.int32)[None, :] + jax.random.randint(ks[2], (BATCH, 1), 0, 4096, dtype=jnp.int32),
        'w_ada': nrm(ks[3], (L, D_MODEL, 6 * D_MODEL), D_MODEL),
        'b_ada': small(ks[4], (L, 6 * D_MODEL)),
        'norm1_g': gain(ks[5], (L, D_MODEL)),
        'w_in': nrm(ks[6], (L, D_MODEL, IN_WIDTH), D_MODEL),
        'conv_w': nrm(ks[7], (L, SSM_CONV, SSM_CONV_DIM), SSM_CONV),
        'conv_b': small(ks[8], (L, SSM_CONV_DIM)),
        'dt_bias': dt0 + jnp.log(-jnp.expm1(-dt0)),
        'a_log': jnp.log(jax.random.uniform(ks[10], (L, SSM_HEADS), f32, minval=1.0, maxval=16.0)),
        'd_skip': 1.0 + 0.1 * jax.random.normal(ks[11], (L, SSM_HEADS), f32),
        'ssm_norm_g': gain(ks[12], (L, SSM_D_INNER)),
        'w_ssm_out': nrm(ks[13], (L, SSM_D_INNER, D_MODEL), SSM_D_INNER),
        'cmp_pe_k': small(ks[14], (L, CMP_BLOCK, NSA_HEAD_DIM)),
        'cmp_w1_k': nrm(ks[15], (L, CMP_BLOCK * NSA_HEAD_DIM, CMP_HIDDEN), CMP_BLOCK * NSA_HEAD_DIM),
        'cmp_w2_k': nrm(ks[16], (L, CMP_HIDDEN, NSA_HEAD_DIM), CMP_HIDDEN),
        'cmp_pe_v': small(ks[17], (L, CMP_BLOCK, NSA_HEAD_DIM)),
        'cmp_w1_v': nrm(ks[18], (L, CMP_BLOCK * NSA_HEAD_DIM, CMP_HIDDEN), CMP_BLOCK * NSA_HEAD_DIM),
        'cmp_w2_v': nrm(ks[19], (L, CMP_HIDDEN, NSA_HEAD_DIM), CMP_HIDDEN),
        'w_nsa_out': nrm(ks[20], (L, NSA_WIDTH, D_MODEL), NSA_WIDTH),
        'w_o': nrm(ks[21], (L, D_MODEL, D_MODEL), D_MODEL),
        'norm2_g': gain(ks[22], (L, D_MODEL)),
        'w_router_group': nrm(ks[23], (L, D_MODEL, MOE_GROUPS), D_MODEL),
        'b_router_group': small(ks[24], (L, MOE_GROUPS), 0.01),
        'w_router_expert': nrm(ks[25], (L, D_MODEL, N_EXPERTS), D_MODEL),
        'b_router_expert': small(ks[26], (L, N_EXPERTS), 0.01),
        'w_exp_gate': nrm(ks[27], (L, N_EXPERTS, D_MODEL, EXPERT_HIDDEN), D_MODEL),
        'w_exp_up': nrm(ks[28], (L, N_EXPERTS, D_MODEL, EXPERT_HIDDEN), D_MODEL),
        'w_exp_down': nrm(ks[29], (L, N_EXPERTS, EXPERT_HIDDEN, D_MODEL), EXPERT_HIDDEN),
        'final_g': gain(ks[30], (D_MODEL,)),
    }


def reference(x, c, positions, w_ada, b_ada, norm1_g, w_in, conv_w, conv_b, dt_bias, a_log, d_skip, ssm_norm_g, w_ssm_out, cmp_pe_k, cmp_w1_k, cmp_w2_k, cmp_pe_v, cmp_w1_v, cmp_w2_v, w_nsa_out, w_o, norm2_g, w_router_group, b_router_group, w_router_expert, b_router_expert, w_exp_gate, w_exp_up, w_exp_down, final_g):
    inv_freq = ROPE_THETA ** (-jnp.arange(0, NSA_HEAD_DIM, 2, dtype=jnp.float32) / NSA_HEAD_DIM)
    ang = positions.astype(jnp.float32)[..., None] * inv_freq
    cos, sin = jnp.cos(ang), jnp.sin(ang)
    h = x
    for l in range(DEPTH):
        h = _hybrid_layer(h, c, cos, sin, w_ada[l], b_ada[l], norm1_g[l], w_in[l], conv_w[l], conv_b[l], dt_bias[l], a_log[l], d_skip[l], ssm_norm_g[l], w_ssm_out[l], cmp_pe_k[l], cmp_w1_k[l], cmp_w2_k[l], cmp_pe_v[l], cmp_w1_v[l], cmp_w2_v[l], w_nsa_out[l], w_o[l], norm2_g[l], w_router_group[l], b_router_group[l], w_router_expert[l], b_router_expert[l], w_exp_gate[l], w_exp_up[l], w_exp_down[l])
    return _rmsnorm(h, final_g)
```

```python
import functools
import math

import numpy as np
import jax
import jax.numpy as jnp
from jax import lax
from jax.experimental import pallas as pl
from jax.experimental.pallas import tpu as pltpu

D_MODEL = 1024
SSM_D_INNER = 2048
SSM_HEAD_DIM = 64
SSM_HEADS = 32
SSM_GROUPS = 4
SSM_HPG = 8
SSM_STATE = 128
SSM_CONV = 4
SSM_CHUNK = 128
SSM_CONV_DIM = 3072
NSA_HEADS = 16
NSA_HEAD_DIM = 64
NSA_WIDTH = 1024
NSA_KV_GROUPS = 2
NSA_HPG = 8
NSA_KV_WIDTH = 128
CMP_BLOCK = 32
CMP_STRIDE = 16
CMP_HIDDEN = 256
SEL_BLOCK = 64
SEL_TOPK = 4
WINDOW = 512
NSA_QBLOCK = 64
ROPE_THETA = 10000.0
MOE_GROUPS = 4
EXPERTS_PER_GROUP = 8
N_EXPERTS = 32
EXPERT_TOPK = 2
EXPERT_HIDDEN = 256
NORM_EPS = 1e-6

LANES = 128
NEG = -1e30
WIN_PAD = 704
KEY_CHUNK = 256
EXPERT_TILE = 512
VMEM_LIMIT = 56 * 1024 * 1024

C_Z, C_X, C_BC, C_Q, C_GS, C_GA, C_KV, C_MAIN = 0, 2048, 4096, 5120, 6144, 7168, 8192, 8960

f32 = jnp.float32
bf16 = jnp.bfloat16


def _cparams(sem):
    return pltpu.CompilerParams(dimension_semantics=sem, vmem_limit_bytes=VMEM_LIMIT)


def _dot(a, b):
    return jnp.dot(a, b, preferred_element_type=f32)


def _dot_nt(a, b):
    return lax.dot_general(a, b, (((1,), (1,)), ((), ())), preferred_element_type=f32)


def _split3(x):
    hi = x.astype(bf16)
    r = x - hi.astype(f32)
    mid = r.astype(bf16)
    lo = (r - mid.astype(f32)).astype(bf16)
    return hi, mid, lo


def _dot_exact_rhs(a, b_bf16):
    hi, mid, lo = _split3(a)
    return _dot(hi, b_bf16) + _dot(mid, b_bf16) + _dot(lo, b_bf16)


def _dot_hp(a, b):
    ah = a.astype(bf16)
    al = (a - ah.astype(f32)).astype(bf16)
    bh = b.astype(bf16)
    bl = (b - bh.astype(f32)).astype(bf16)
    return _dot(ah, bh) + _dot(al, bh) + _dot(ah, bl)


def _silu(x):
    return x * jax.nn.sigmoid(x)


def _ada_kernel(c_ref, w_ref, b_ref, o_ref):
    o_ref[...] = _dot_hp(_silu(c_ref[...]), w_ref[...]) + b_ref[...]


def _ada(c, w_ada, b_ada):
    b, d = c.shape
    n = w_ada.shape[1]
    tn = 1024
    return pl.pallas_call(
        _ada_kernel,
        grid=(n // tn,),
        in_specs=[pl.BlockSpec((b, d), lambda j: (0, 0)),
                  pl.BlockSpec((d, tn), lambda j: (0, j)),
                  pl.BlockSpec((1, tn), lambda j: (0, j))],
        out_specs=pl.BlockSpec((b, tn), lambda j: (0, j)),
        out_shape=jax.ShapeDtypeStruct((b, n), f32),
        compiler_params=_cparams(("arbitrary",)),
    )(c, w_ada, b_ada.reshape(1, n))


def _rope_kernel(pos_ref, inv_ref, sgn_ref, cos_ref, sin_ref):
    ang = pos_ref[...].astype(f32) * inv_ref[...]
    cos_ref[...] = jnp.cos(ang)
    sin_ref[...] = jnp.sin(ang) * sgn_ref[...]


def _rope_tables(positions):
    t = positions.size
    tm = 2048
    inv = ROPE_THETA ** (-jnp.arange(0, NSA_HEAD_DIM, 2, dtype=f32) / NSA_HEAD_DIM)
    inv_full = jnp.tile(inv, LANES // 32).reshape(1, LANES)
    lane = np.arange(LANES)
    sgn = jnp.asarray(np.where((lane % 64) < 32, -1.0, 1.0), f32).reshape(1, LANES)
    return pl.pallas_call(
        _rope_kernel,
        grid=(t // tm,),
        in_specs=[pl.BlockSpec((tm, 1), lambda i: (i, 0)),
                  pl.BlockSpec((1, LANES), lambda i: (0, 0)),
                  pl.BlockSpec((1, LANES), lambda i: (0, 0))],
        out_specs=[pl.BlockSpec((tm, LANES), lambda i: (i, 0))] * 2,
        out_shape=[jax.ShapeDtypeStruct((t, LANES), f32)] * 2,
        compiler_params=_cparams(("arbitrary",)),
    )(positions.reshape(t, 1), inv_full, sgn)


def _hn_kernel(x_ref, ada_ref, g_ref, ws_ref, hn_ref, small_ref):
    x = x_ref[...]
    y = x * lax.rsqrt(jnp.mean(x * x, axis=-1, keepdims=True) + NORM_EPS) * g_ref[...]
    ada = ada_ref[0]
    hn = y * (1.0 + ada[1:2]) + ada[0:1]
    hn_ref[...] = hn.astype(bf16)
    small_ref[...] = _dot_hp(hn, ws_ref[...])


def _hn(x2, ada3, norm1_g, w_small, seq):
    t, d = x2.shape
    tm = 512
    return pl.pallas_call(
        _hn_kernel,
        grid=(t // tm,),
        in_specs=[pl.BlockSpec((tm, d), lambda i: (i, 0)),
                  pl.BlockSpec((1, 6, d), lambda i: ((i * tm) // seq, 0, 0)),
                  pl.BlockSpec((1, d), lambda i: (0, 0)),
                  pl.BlockSpec((d, LANES), lambda i: (0, 0))],
        out_specs=[pl.BlockSpec((tm, d), lambda i: (i, 0)),
                   pl.BlockSpec((tm, LANES), lambda i: (i, 0))],
        out_shape=[jax.ShapeDtypeStruct((t, d), bf16),
                   jax.ShapeDtypeStruct((t, LANES), f32)],
        compiler_params=_cparams(("arbitrary",)),
    )(x2, ada3, norm1_g.reshape(1, d), w_small)


def _mm_kernel(a_ref, b_ref, o_ref):
    o_ref[...] = _dot(a_ref[...], b_ref[...]).astype(o_ref.dtype)


def _inproj(hn, w_main):
    t, d = hn.shape
    n = w_main.shape[1]
    tm = 1024 if t % 1024 == 0 else t
    tn = 1792
    return pl.pallas_call(
        _mm_kernel,
        grid=(t // tm, n // tn),
        in_specs=[pl.BlockSpec((tm, d), lambda i, j: (i, 0)),
                  pl.BlockSpec((d, tn), lambda i, j: (0, j))],
        out_specs=pl.BlockSpec((tm, tn), lambda i, j: (i, j)),
        out_shape=jax.ShapeDtypeStruct((t, n), bf16),
        compiler_params=_cparams(("arbitrary", "arbitrary")),
    )(hn, w_main)


def _softplus(x):
    return jnp.maximum(x, 0.0) + jnp.log1p(jnp.exp(-jnp.abs(x)))


def _ssd_kernel(z_ref, xs_ref, bc_ref, sm_ref, cw_ref, cb_ref, dtb_ref, alog_ref, dsk_ref, ng_ref, e_ref,
                o_ref, xbuf, bcbuf, state, ybuf):
    q = SSM_CHUNK
    n_st = SSM_STATE
    gw = SSM_HPG * SSM_HEAD_DIM

    @pl.when(pl.program_id(1) == 0)
    def _():
        xbuf[0:8, :] = jnp.zeros((8, SSM_D_INNER), f32)
        bcbuf[0:8, :] = jnp.zeros((8, 2 * SSM_GROUPS * n_st), f32)
        state[...] = jnp.zeros(state.shape, f32)

    xbuf[8:8 + q, :] = xs_ref[...].astype(f32)
    bcbuf[8:8 + q, :] = bc_ref[...].astype(f32)

    def conv(buf, lo, hi):
        acc = cb_ref[:, lo:hi] + cw_ref[0:1, lo:hi] * buf[pl.ds(5, q), :]
        for k in range(1, SSM_CONV):
            acc = acc + cw_ref[k:k + 1, lo:hi] * buf[pl.ds(5 + k, q), :]
        return _silu(acc)

    xc = conv(xbuf, 0, SSM_D_INNER)
    bcm = conv(bcbuf, SSM_D_INNER, SSM_CONV_DIM)
    xbuf[0:8, :] = xbuf[q:q + 8, :]
    bcbuf[0:8, :] = bcbuf[q:q + 8, :]

    lane = lax.broadcasted_iota(jnp.int32, (1, LANES), 1)
    dt = _softplus(sm_ref[...] + dtb_ref[...])
    a = jnp.where(lane < SSM_HEADS, -jnp.exp(alog_ref[...]), 0.0)
    da = dt * a
    ri = lax.broadcasted_iota(jnp.int32, (q, q), 0)
    ci = lax.broadcasted_iota(jnp.int32, (q, q), 1)
    tril = ci <= ri
    tril_b = jnp.where(tril, 1.0, 0.0).astype(bf16)
    acum = _dot_exact_rhs_left(tril_b, da)
    acum_t = acum.T
    a_last = acum[q - 1:q, :]
    exp_a = jnp.exp(acum)
    w_end = jnp.exp(a_last - acum) * dt
    e_mat = e_ref[...]
    dt_x = _dot(dt.astype(bf16), e_mat)
    expa_x = _dot(exp_a.astype(bf16), e_mat)
    wend_x = _dot(w_end.astype(bf16), e_mat)
    elast_x = _dot_exact_rhs(jnp.broadcast_to(jnp.exp(a_last), (8, LANES)), e_mat)[0:1, :]
    xdt = (xc * dt_x).astype(bf16)
    xw = (xc * wend_x).astype(bf16)
    lo_mask = lane < SSM_HEAD_DIM

    for g in range(SSM_GROUPS):
        bg = bcm[:, g * n_st:(g + 1) * n_st]
        cg = bcm[:, SSM_GROUPS * n_st + g * n_st:SSM_GROUPS * n_st + (g + 1) * n_st]
        bg_b = bg.astype(bf16)
        cg_b = cg.astype(bf16)
        cb = _dot_nt(cg_b, bg_b)
        st = state[g]
        yoff = _dot(cg_b, st.astype(bf16)) * expa_x[:, g * gw:(g + 1) * gw]
        for hp in range(SSM_HPG // 2):
            pair = g * (SSM_HPG // 2) + hp
            xpair = xdt[:, pair * LANES:(pair + 1) * LANES]
            acc = yoff[:, hp * LANES:(hp + 1) * LANES]
            for par in range(2):
                h = 2 * pair + par
                xx = jnp.where(lo_mask, xpair, 0.0) if par == 0 else jnp.where(lo_mask, 0.0, xpair)
                seg = acum[:, h:h + 1] - acum_t[h:h + 1, :]
                decay = jnp.exp(jnp.where(tril, seg, NEG))
                acc = acc + _dot((cb * decay).astype(bf16), xx.astype(bf16))
            ybuf[:, pair * LANES:(pair + 1) * LANES] = acc
        state[g] = st * elast_x[:, g * gw:(g + 1) * gw] + _dot(bg.T.astype(bf16), xw[:, g * gw:(g + 1) * gw])

    z = z_ref[...].astype(f32)
    y = (ybuf[...] + dsk_ref[...] * xc) * _silu(z)
    for g in range(SSM_GROUPS):
        yg = y[:, g * gw:(g + 1) * gw]
        ms = jnp.mean(yg * yg, axis=-1, keepdims=True)
        o_ref[:, g * gw:(g + 1) * gw] = (yg * lax.rsqrt(ms + NORM_EPS) * ng_ref[:, g * gw:(g + 1) * gw]).astype(bf16)


def _dot_exact_rhs_left(l_bf16, x):
    hi, mid, lo = _split3(x)
    return _dot(l_bf16, hi) + _dot(l_bf16, mid) + _dot(l_bf16, lo)


def _ssd(main, small, conv_w, conv_b, dt_bias, a_log, d_skip, ssm_norm_g, batch, seq):
    t = main.shape[0]
    q = SSM_CHUNK
    nc = seq // q
    pad = LANES - SSM_HEADS
    dtb = jnp.pad(dt_bias, (0, pad)).reshape(1, LANES)
    alog = jnp.pad(a_log, (0, pad)).reshape(1, LANES)
    dsk = jnp.repeat(d_skip, SSM_HEAD_DIM).reshape(1, SSM_D_INNER)
    e_np = (np.arange(LANES)[:, None] == (np.arange(SSM_D_INNER)[None, :] // SSM_HEAD_DIM)).astype(np.float32)
    e_mat = jnp.asarray(e_np, bf16)
    row = lambda b, c: (b * nc + c)
    const = lambda b, c: (0, 0)
    return pl.pallas_call(
        _ssd_kernel,
        grid=(batch, nc),
        in_specs=[pl.BlockSpec((q, SSM_D_INNER), lambda b, c: (row(b, c), C_Z // SSM_D_INNER)),
                  pl.BlockSpec((q, SSM_D_INNER), lambda b, c: (row(b, c), C_X // SSM_D_INNER)),
                  pl.BlockSpec((q, 1024), lambda b, c: (row(b, c), C_BC // 1024)),
                  pl.BlockSpec((q, LANES), lambda b, c: (row(b, c), 0)),
                  pl.BlockSpec((SSM_CONV, SSM_CONV_DIM), const),
                  pl.BlockSpec((1, SSM_CONV_DIM), const),
                  pl.BlockSpec((1, LANES), const),
                  pl.BlockSpec((1, LANES), const),
                  pl.BlockSpec((1, SSM_D_INNER), const),
                  pl.BlockSpec((1, SSM_D_INNER), const),
                  pl.BlockSpec((LANES, SSM_D_INNER), const)],
        out_specs=pl.BlockSpec((q, SSM_D_INNER), lambda b, c: (row(b, c), 0)),
        out_shape=jax.ShapeDtypeStruct((t, SSM_D_INNER), bf16),
        scratch_shapes=[pltpu.VMEM((q + 8, SSM_D_INNER), f32),
                        pltpu.VMEM((q + 8, 2 * SSM_GROUPS * SSM_STATE), f32),
                        pltpu.VMEM((SSM_GROUPS, SSM_STATE, SSM_HPG * SSM_HEAD_DIM), f32),
                        pltpu.VMEM((q, SSM_D_INNER), f32)],
        compiler_params=_cparams(("arbitrary", "arbitrary")),
    )(main, main, main, small, conv_w, conv_b.reshape(1, -1), dtb, alog, dsk,
      ssm_norm_g.reshape(1, -1), e_mat)


def _flash_step(sb, v, m, l, acc):
    m_new = jnp.maximum(m, jnp.max(sb, axis=-1, keepdims=True))
    alpha = jnp.exp(m - m_new)
    p = jnp.exp(sb - m_new)
    l_new = alpha * l + jnp.sum(p, axis=-1, keepdims=True)
    acc_new = alpha * acc + _dot(p.astype(bf16), v)
    return m_new, l_new, acc_new


def _nsa_kernel(q_ref, kc_ref, vc_ref, ks_ref, vs_ref, kw_ref, vw_ref, sm_ref, cos_ref, sin_ref,
                pek_ref, w1k_ref, w2k_ref, pev_ref, w1v_ref, w2v_ref, ovl_ref, eg_ref,
                o_ref,
                ks_p, vs_p, kw_p, vw_p, kc_buf, vc_buf, kcmp_p, vcmp_p):
    g = pl.program_id(1)
    seq = q_ref.shape[0]
    qb = NSA_QBLOCK
    dk = NSA_HEAD_DIM
    n_cmp = seq // CMP_STRIDE
    n_pairs = NSA_HPG // 2
    rows = NSA_HPG * qb
    scale = dk ** -0.5

    lane = lax.broadcasted_iota(jnp.int32, (1, LANES), 1)
    lo = lane < dk
    first_half = (lane & (dk - 1)) < dk // 2

    def rope(tv, cs, sn):
        rot = jnp.where(first_half, pltpu.roll(tv, LANES - dk // 2, 1), pltpu.roll(tv, dk // 2, 1))
        return tv * cs + rot * sn

    def group_lanes(tv):
        return jnp.where(lo, jnp.where(g == 0, tv, pltpu.roll(tv, dk, 1)), 0.0)

    prep_rows = 256

    def prep(i, carry):
        r0 = pl.multiple_of(i * prep_rows, prep_rows)
        sl = pl.ds(r0, prep_rows)
        cs = cos_ref[sl, :]
        sn = sin_ref[sl, :]
        ks_p[sl, :] = group_lanes(rope(ks_ref[sl, :].astype(f32), cs, sn)).astype(bf16)
        vs_p[sl, :] = group_lanes(vs_ref[sl, :].astype(f32)).astype(bf16)
        slw = pl.ds(pl.multiple_of(r0 + WIN_PAD, 64), prep_rows)
        kw_p[slw, :] = group_lanes(rope(kw_ref[sl, :].astype(f32), cs, sn)).astype(bf16)
        vw_p[slw, :] = group_lanes(vw_ref[sl, :].astype(f32)).astype(bf16)
        kc_buf[sl, :] = group_lanes(rope(kc_ref[sl, :].astype(f32), cs, sn))
        vc_buf[sl, :] = group_lanes(vc_ref[sl, :].astype(f32))
        return carry

    lax.fori_loop(0, seq // prep_rows, prep, 0)
    kw_p[0:WIN_PAD, :] = jnp.zeros((WIN_PAD, LANES), bf16)
    vw_p[0:WIN_PAD, :] = jnp.zeros((WIN_PAD, LANES), bf16)
    kc_buf[seq:seq + CMP_BLOCK, :] = jnp.zeros((CMP_BLOCK, LANES), f32)
    vc_buf[seq:seq + CMP_BLOCK, :] = jnp.zeros((CMP_BLOCK, LANES), f32)

    def compress(buf, pe_ref, w1_ref, w2_ref, out):
        pre = jnp.zeros((n_cmp, CMP_HIDDEN), f32)
        for l in range(CMP_BLOCK):
            xl = buf[pl.ds(l, n_cmp, stride=CMP_STRIDE), :] + pe_ref[l:l + 1, :]
            pre = pre + _dot(xl.astype(bf16), w1_ref[l * LANES:(l + 1) * LANES, :])
        out[...] = _dot(_silu(pre).astype(bf16), w2_ref[...]).astype(bf16)

    compress(kc_buf, pek_ref, w1k_ref, w2k_ref, kcmp_p)
    compress(vc_buf, pev_ref, w1v_ref, w2v_ref, vcmp_p)

    def tile8(x):
        return jnp.concatenate([x] * NSA_HPG, axis=0)

    wr = lax.broadcasted_iota(jnp.int32, (qb, KEY_CHUNK), 0)
    wc = lax.broadcasted_iota(jnp.int32, (qb, KEY_CHUNK), 1)

    def qblock(qi, carry):
        s0 = pl.multiple_of(qi * qb, qb)
        sl = pl.ds(s0, qb)
        tq = s0 + lax.broadcasted_iota(jnp.int32, (qb, 1), 0)

        q2 = q_ref[sl, :].astype(f32)
        qs = jnp.concatenate([q2[:, p * LANES:(p + 1) * LANES] for p in range(n_pairs)], axis=0)
        cs = jnp.concatenate([cos_ref[sl, :]] * n_pairs, axis=0)
        sn = jnp.concatenate([sin_ref[sl, :]] * n_pairs, axis=0)
        qr = rope(qs, cs, sn) * scale
        q_even = jnp.where(lo, qr, 0.0)
        q_odd = jnp.where(lo, pltpu.roll(qr, dk, 1), 0.0)
        qall = jnp.concatenate([q_even, q_odd], axis=0).astype(bf16)

        sc = _dot_nt(qall, kcmp_p[...])
        ccol = lax.broadcasted_iota(jnp.int32, (qb, n_cmp), 1)
        cmask = (ccol * CMP_STRIDE + (CMP_BLOCK - 1)) <= tq
        cbias = tile8(jnp.where(cmask, 0.0, NEG))
        cmf = tile8(jnp.where(cmask, 1.0, 0.0))
        scb = sc + cbias
        pc = jnp.exp(scb - jnp.max(scb, axis=-1, keepdims=True)) * cmf
        pc = pc / jnp.maximum(jnp.sum(pc, axis=-1, keepdims=True), 1e-30)
        o_cmp = _dot(pc.astype(bf16), vcmp_p[...])
        pcs = pc[0:qb]
        for r in range(1, NSA_HPG):
            pcs = pcs + pc[r * qb:(r + 1) * qb]
        imp = _dot_exact_rhs(pcs, ovl_ref[...])

        jl = lax.broadcasted_iota(jnp.int32, (qb, LANES), 1)
        prio = jnp.where(jl == qi, 1e6, jnp.where(jl > qi, -jnp.inf, jnp.where(jl == 0, 5e5, 0.0)))
        val = imp + prio
        sel = jnp.zeros((qb, LANES), f32)
        for _ in range(SEL_TOPK):
            mx = jnp.max(val, axis=-1, keepdims=True)
            idx = jnp.min(jnp.where(val == mx, jl, LANES), axis=-1, keepdims=True)
            hit = jl == idx
            sel = jnp.where(hit, 1.0, sel)
            val = jnp.where(hit, -jnp.inf, val)
        sel_b = sel.astype(bf16)

        m0 = jnp.full((rows, 1), NEG, f32)
        l0 = jnp.zeros((rows, 1), f32)
        a0 = jnp.zeros((rows, LANES), f32)

        er = lax.broadcasted_iota(jnp.int32, (LANES, KEY_CHUNK), 0)
        ec = lax.broadcasted_iota(jnp.int32, (LANES, KEY_CHUNK), 1)

        def slc_chunk(c, mla):
            k0 = pl.multiple_of(c * KEY_CHUNK, KEY_CHUNK)
            kk = ks_p[pl.ds(k0, KEY_CHUNK), :]
            vv = vs_p[pl.ds(k0, KEY_CHUNK), :]
            s = _dot_nt(qall, kk)
            expand = jnp.where(er == c * (KEY_CHUNK // SEL_BLOCK) + (ec >> (SEL_BLOCK.bit_length() - 1)),
                               1.0, 0.0).astype(bf16)
            selx = _dot(sel_b, expand)
            ok = (selx > 0.5) & ((k0 + wc) <= tq)
            return _flash_step(s + tile8(jnp.where(ok, 0.0, NEG)), vv, *mla)

        n_chunks = qi // (KEY_CHUNK // SEL_BLOCK) + 1
        _, l_s, acc_s = lax.fori_loop(0, n_chunks, slc_chunk, (m0, l0, a0))
        o_slc = acc_s / l_s

        mla = (m0, l0, a0)
        for j in (2, 1, 0):
            b0 = pl.multiple_of(s0 + j * KEY_CHUNK, 64)
            kk = kw_p[pl.ds(b0, KEY_CHUNK), :]
            vv = vw_p[pl.ds(b0, KEY_CHUNK), :]
            s = _dot_nt(qall, kk)
            rel = wc + (j * KEY_CHUNK - WIN_PAD) - wr
            ok = (rel <= 0) & (rel > -WINDOW) & ((wc + (j * KEY_CHUNK - WIN_PAD) + s0) >= 0)
            mla = _flash_step(s + tile8(jnp.where(ok, 0.0, NEG)), vv, *mla)
        o_win = mla[2] / mla[1]

        gates = _dot(jax.nn.sigmoid(sm_ref[sl, :]).astype(bf16), eg_ref[0])
        half = n_pairs * qb

        def pair_tile(o, p):
            return o[p * qb:(p + 1) * qb] + pltpu.roll(o[half + p * qb:half + (p + 1) * qb], dk, 1)

        width = NSA_HPG * dk
        for p in range(n_pairs):
            cols = slice(p * LANES, (p + 1) * LANES)
            out = (gates[:, 0 * width + p * LANES:0 * width + (p + 1) * LANES] * pair_tile(o_cmp, p)
                   + gates[:, 1 * width + p * LANES:1 * width + (p + 1) * LANES] * pair_tile(o_slc, p)
                   + gates[:, 2 * width + p * LANES:2 * width + (p + 1) * LANES] * pair_tile(o_win, p))
            o_ref[sl, cols] = out.astype(bf16)
        return carry

    lax.fori_loop(0, seq // qb, qblock, 0)


def _pad_w1(w1):
    h = w1.shape[1]
    w = w1.reshape(CMP_BLOCK, NSA_HEAD_DIM, h)
    return jnp.pad(w, ((0, 0), (0, LANES - NSA_HEAD_DIM), (0, 0))).reshape(CMP_BLOCK * LANES, h).astype(bf16)


def _nsa(main, small, cos_t, sin_t, cmp_pe_k, cmp_w1_k, cmp_w2_k, cmp_pe_v, cmp_w1_v, cmp_w2_v, batch, seq):
    t = main.shape[0]
    dk = NSA_HEAD_DIM
    width = NSA_HPG * dk
    n_cmp = seq // CMP_STRIDE
    pad_l = ((0, 0), (0, LANES - dk))
    pek = jnp.pad(cmp_pe_k, pad_l)
    pev = jnp.pad(cmp_pe_v, pad_l)
    w1k = _pad_w1(cmp_w1_k)
    w1v = _pad_w1(cmp_w1_v)
    w2k = jnp.pad(cmp_w2_k, pad_l).astype(bf16)
    w2v = jnp.pad(cmp_w2_v, pad_l).astype(bf16)
    cstart = np.arange(LANES)[:, None] * CMP_STRIDE
    sstart = np.arange(LANES)[None, :] * SEL_BLOCK
    valid_c = np.arange(LANES)[:, None] < (seq - CMP_BLOCK) // CMP_STRIDE + 1
    valid_j = np.arange(LANES)[None, :] < seq // SEL_BLOCK
    ovl = ((cstart < sstart + SEL_BLOCK) & (cstart + CMP_BLOCK > sstart) & valid_c & valid_j).astype(np.float32)
    ovl = jnp.asarray(ovl[:n_cmp] if n_cmp <= LANES else ovl, bf16)
    eg = np.zeros((NSA_KV_GROUPS, LANES, 3 * width), np.float32)
    for gg in range(NSA_KV_GROUPS):
        for r in range(NSA_HPG):
            for j in range(3):
                eg[gg, SSM_HEADS + gg * NSA_HPG * 3 + r * 3 + j, j * width + r * dk:j * width + (r + 1) * dk] = 1.0
    eg = jnp.asarray(eg, bf16)

    kv_blk = C_KV // LANES
    kv_spec = lambda j: pl.BlockSpec((seq, LANES), lambda b, g: (b, kv_blk + j))
    const2 = lambda b, g: (0, 0)
    return pl.pallas_call(
        _nsa_kernel,
        grid=(batch, NSA_KV_GROUPS),
        in_specs=[pl.BlockSpec((seq, width), lambda b, g: (b, C_Q // width + g))]
                 + [kv_spec(j) for j in range(6)]
                 + [pl.BlockSpec((seq, LANES), lambda b, g: (b, 0))] * 3
                 + [pl.BlockSpec((CMP_BLOCK, LANES), const2),
                    pl.BlockSpec((CMP_BLOCK * LANES, CMP_HIDDEN), const2),
                    pl.BlockSpec((CMP_HIDDEN, LANES), const2)] * 2
                 + [pl.BlockSpec(ovl.shape, const2),
                    pl.BlockSpec((1, LANES, 3 * width), lambda b, g: (g, 0, 0))],
        out_specs=pl.BlockSpec((seq, width), lambda b, g: (b, g)),
        out_shape=jax.ShapeDtypeStruct((t, NSA_WIDTH), bf16),
        scratch_shapes=[pltpu.VMEM((seq, LANES), bf16),
                        pltpu.VMEM((seq, LANES), bf16),
                        pltpu.VMEM((seq + WIN_PAD, LANES), bf16),
                        pltpu.VMEM((seq + WIN_PAD, LANES), bf16),
                        pltpu.VMEM((seq + CMP_BLOCK, LANES), f32),
                        pltpu.VMEM((seq + CMP_BLOCK, LANES), f32),
                        pltpu.VMEM((n_cmp, LANES), bf16),
                        pltpu.VMEM((n_cmp, LANES), bf16)],
        compiler_params=_cparams(("arbitrary", "arbitrary")),
    )(main, main, main, main, main, main, main, small, cos_t, sin_t,
      pek, w1k, w2k, pev, w1v, w2v, ovl, eg)


def _post_kernel(yn_ref, on_ref, gs_ref, ga_ref, x_ref, ada_ref, wso_ref, wno_ref, wo_ref, g2_ref, wr_ref, br_ref,
                 h1_ref, hn2_ref, route_ref, cnt_ref, carry):
    tm = x_ref.shape[0]

    @pl.when(pl.program_id(0) == 0)
    def _():
        carry[...] = jnp.zeros(carry.shape, f32)

    y_ssm = _dot(yn_ref[...], wso_ref[...])
    y_nsa = _dot(on_ref[...], wno_ref[...])
    merged = jax.nn.sigmoid(gs_ref[...].astype(f32)) * y_ssm + jax.nn.sigmoid(ga_ref[...].astype(f32)) * y_nsa
    ada = ada_ref[0]
    h1 = x_ref[...] + ada[2:3] * _dot(merged.astype(bf16), wo_ref[...])
    h1_ref[...] = h1
    hn2 = h1 * lax.rsqrt(jnp.mean(h1 * h1, axis=-1, keepdims=True) + NORM_EPS) * g2_ref[...]
    hn2 = hn2 * (1.0 + ada[4:5]) + ada[3:4]
    hn2_ref[...] = hn2

    logits = _dot_hp(hn2, wr_ref[...]) + br_ref[...]
    lane = lax.broadcasted_iota(jnp.int32, (tm, LANES), 1)
    is_g = lane < MOE_GROUPS
    lg = jnp.where(is_g, logits, NEG)
    eg = jnp.where(is_g, jnp.exp(lg - jnp.max(lg, axis=-1, keepdims=True)), 0.0)
    pg_all = eg / jnp.sum(eg, axis=-1, keepdims=True)
    pg = jnp.max(pg_all, axis=-1, keepdims=True)
    gsel = jnp.min(jnp.where(is_g & (pg_all == pg), lane, LANES), axis=-1, keepdims=True)
    member = (lane >= MOE_GROUPS) & (lane < MOE_GROUPS + N_EXPERTS) & (((lane - MOE_GROUPS) >> 3) == gsel)
    le = jnp.where(member, logits, NEG)
    ee = jnp.where(member, jnp.exp(le - jnp.max(le, axis=-1, keepdims=True)), 0.0)
    pe = jnp.where(member, ee / jnp.sum(ee, axis=-1, keepdims=True), -1.0)
    v0 = jnp.max(pe, axis=-1, keepdims=True)
    i0 = jnp.min(jnp.where(pe == v0, lane, LANES), axis=-1, keepdims=True)
    hit0 = lane == i0
    pe1 = jnp.where(hit0, -1.0, pe)
    v1 = jnp.max(pe1, axis=-1, keepdims=True)
    i1 = jnp.min(jnp.where(pe1 == v1, lane, LANES), axis=-1, keepdims=True)
    hit1 = lane == i1
    den = v0 + v1
    w0 = pg * v0 / den
    w1 = pg * v1 / den

    both = jnp.where(hit0 | hit1, 1.0, 0.0)
    ri = lax.broadcasted_iota(jnp.int32, (tm, tm), 0)
    ci = lax.broadcasted_iota(jnp.int32, (tm, tm), 1)
    strict = jnp.where(ci < ri, 1.0, 0.0).astype(bf16)
    base = carry[0:1, :] + _dot(strict, both.astype(bf16))
    rank0 = jnp.sum(jnp.where(hit0, base, 0.0), axis=-1, keepdims=True)
    rank1 = jnp.sum(jnp.where(hit1, base, 0.0), axis=-1, keepdims=True)
    total = carry[...] + jnp.sum(both, axis=0, keepdims=True)
    carry[...] = total
    cnt_ref[...] = total

    e0 = (i0 - MOE_GROUPS).astype(f32)
    e1 = (i1 - MOE_GROUPS).astype(f32)
    route = jnp.where(lane == 0, e0, jnp.where(lane == 1, e1, jnp.where(lane == 2, w0, jnp.where(
        lane == 3, w1, jnp.where(lane == 4, rank0, jnp.where(lane == 5, rank1, 0.0))))))
    route_ref[...] = route


def _post(yn, o_nsa, main, x2, ada3, w_ssm_out, w_nsa_out, w_o, norm2_g, w_route, b_route, seq):
    t, d = x2.shape
    tm = 512
    const = lambda i: (0, 0)
    return pl.pallas_call(
        _post_kernel,
        grid=(t // tm,),
        in_specs=[pl.BlockSpec((tm, SSM_D_INNER), lambda i: (i, 0)),
                  pl.BlockSpec((tm, NSA_WIDTH), lambda i: (i, 0)),
                  pl.BlockSpec((tm, d), lambda i: (i, C_GS // d)),
                  pl.BlockSpec((tm, d), lambda i: (i, C_GA // d)),
                  pl.BlockSpec((tm, d), lambda i: (i, 0)),
                  pl.BlockSpec((1, 6, d), lambda i: ((i * tm) // seq, 0, 0)),
                  pl.BlockSpec((SSM_D_INNER, d), const),
                  pl.BlockSpec((NSA_WIDTH, d), const),
                  pl.BlockSpec((d, d), const),
                  pl.BlockSpec((1, d), const),
                  pl.BlockSpec((d, LANES), const),
                  pl.BlockSpec((1, LANES), const)],
        out_specs=[pl.BlockSpec((tm, d), lambda i: (i, 0)),
                   pl.BlockSpec((tm, d), lambda i: (i, 0)),
                   pl.BlockSpec((tm, LANES), lambda i: (i, 0)),
                   pl.BlockSpec((8, LANES), const)],
        out_shape=[jax.ShapeDtypeStruct((t, d), f32),
                   jax.ShapeDtypeStruct((t, d), f32),
                   jax.ShapeDtypeStruct((t, LANES), f32),
                   jax.ShapeDtypeStruct((8, LANES), f32)],
        scratch_shapes=[pltpu.VMEM((8, LANES), f32)],
        compiler_params=_cparams(("arbitrary",)),
    )(yn, o_nsa, main, main, x2, ada3, w_ssm_out, w_nsa_out, w_o, norm2_g.reshape(1, d), w_route, b_route)


def _dispatch_kernel(pos_hbm, src_hbm, init_hbm, dst_hbm, pos_smem, sem_idx, sem):
    del init_hbm
    i = pl.program_id(0)
    tm = pos_smem.shape[0] // EXPERT_TOPK
    idx_cp = pltpu.make_async_copy(pos_hbm.at[i], pos_smem, sem_idx)
    idx_cp.start()
    idx_cp.wait()

    def row_copy(t, p):
        return pltpu.make_async_copy(src_hbm.at[pl.ds(t, 1)], dst_hbm.at[pl.ds(p, 1)], sem)

    def issue(r, carry):
        for k in range(EXPERT_TOPK):
            row_copy(i * tm + r, pos_smem[k * tm + r]).start()
        return carry

    lax.fori_loop(0, tm, issue, 0)

    def drain(r, carry):
        for k in range(EXPERT_TOPK):
            row_copy(0, 0).wait()
        return carry

    lax.fori_loop(0, tm, drain, 0)


def _dispatch(pos_tiles, hn2, n_rows):
    t, d = hn2.shape
    n_steps, two_tm = pos_tiles.shape
    zeros = jnp.zeros((n_rows, d), f32)
    return pl.pallas_call(
        _dispatch_kernel,
        grid=(n_steps,),
        in_specs=[pl.BlockSpec(memory_space=pl.ANY)] * 3,
        out_specs=pl.BlockSpec(memory_space=pl.ANY),
        out_shape=jax.ShapeDtypeStruct((n_rows, d), f32),
        scratch_shapes=[pltpu.SMEM((two_tm,), jnp.int32),
                        pltpu.SemaphoreType.DMA,
                        pltpu.SemaphoreType.DMA],
        input_output_aliases={2: 0},
        compiler_params=_cparams(("arbitrary",)),
    )(pos_tiles, hn2, zeros)


def _expert_kernel(te_ref, nu_ref, xs_ref, wg_ref, wu_ref, wd_ref, y_ref):
    del te_ref
    i = pl.program_id(0)

    @pl.when(i < nu_ref[0])
    def _():
        x = xs_ref[...].astype(bf16)
        hg = _dot(x, wg_ref[0])
        hu = _dot(x, wu_ref[0])
        y_ref[...] = _dot((_silu(hg) * hu).astype(bf16), wd_ref[0])

    @pl.when(i >= nu_ref[0])
    def _():
        y_ref[...] = jnp.zeros(y_ref.shape, f32)


def _experts(tile_expert, n_used, xs, w_gate, w_up, w_down):
    n_rows, d = xs.shape
    n_tiles = n_rows // EXPERT_TILE
    hdim = w_gate.shape[-1]
    grid_spec = pltpu.PrefetchScalarGridSpec(
        num_scalar_prefetch=2,
        grid=(n_tiles,),
        in_specs=[pl.BlockSpec((EXPERT_TILE, d), lambda i, te, nu: (jnp.minimum(i, nu[0] - 1), 0)),
                  pl.BlockSpec((1, d, hdim), lambda i, te, nu: (te[i], 0, 0)),
                  pl.BlockSpec((1, d, hdim), lambda i, te, nu: (te[i], 0, 0)),
                  pl.BlockSpec((1, hdim, d), lambda i, te, nu: (te[i], 0, 0))],
        out_specs=pl.BlockSpec((EXPERT_TILE, d), lambda i, te, nu: (i, 0)),
    )
    return pl.pallas_call(
        _expert_kernel,
        grid_spec=grid_spec,
        out_shape=jax.ShapeDtypeStruct((n_rows, d), f32),
        compiler_params=_cparams(("arbitrary",)),
    )(tile_expert, n_used, xs, w_gate, w_up, w_down)


def _combine_kernel(pos_hbm, ys_hbm, h1_ref, route_ref, ada_ref, fg_ref, o_ref, pos_smem, buf, sem_idx, sem):
    i = pl.program_id(0)
    tm = h1_ref.shape[0]
    idx_cp = pltpu.make_async_copy(pos_hbm.at[i], pos_smem, sem_idx)
    idx_cp.start()
    idx_cp.wait()

    def row_copy(p, k, r):
        return pltpu.make_async_copy(ys_hbm.at[pl.ds(p, 1)], buf.at[k, pl.ds(r, 1)], sem)

    def issue(r, carry):
        for k in range(EXPERT_TOPK):
            row_copy(pos_smem[k * tm + r], k, r).start()
        return carry

    lax.fori_loop(0, tm, issue, 0)

    def drain(r, carry):
        for k in range(EXPERT_TOPK):
            row_copy(0, k, 0).wait()
        return carry

    lax.fori_loop(0, tm, drain, 0)

    route = route_ref[...]
    moe = route[:, 2:3] * buf[0] + route[:, 3:4] * buf[1]
    h2 = h1_ref[...] + ada_ref[0][5:6] * moe
    o_ref[...] = h2 * lax.rsqrt(jnp.mean(h2 * h2, axis=-1, keepdims=True) + NORM_EPS) * fg_ref[...]


def _combine(pos_tiles, ys, h1, route, ada3, final_g, seq):
    t, d = h1.shape
    n_steps, two_tm = pos_tiles.shape
    tm = two_tm // EXPERT_TOPK
    return pl.pallas_call(
        _combine_kernel,
        grid=(n_steps,),
        in_specs=[pl.BlockSpec(memory_space=pl.ANY),
                  pl.BlockSpec(memory_space=pl.ANY),
                  pl.BlockSpec((tm, d), lambda i: (i, 0)),
                  pl.BlockSpec((tm, LANES), lambda i: (i, 0)),
                  pl.BlockSpec((1, 6, d), lambda i: ((i * tm) // seq, 0, 0)),
                  pl.BlockSpec((1, d), lambda i: (0, 0))],
        out_specs=pl.BlockSpec((tm, d), lambda i: (i, 0)),
        out_shape=jax.ShapeDtypeStruct((t, d), f32),
        scratch_shapes=[pltpu.SMEM((two_tm,), jnp.int32),
                        pltpu.VMEM((EXPERT_TOPK, tm, d), f32),
                        pltpu.SemaphoreType.DMA,
                        pltpu.SemaphoreType.DMA],
        compiler_params=_cparams(("arbitrary",)),
    )(pos_tiles, ys, h1, route, ada3, final_g.reshape(1, d))


def _routing_tables(route, cnt, t, tm):
    counts = cnt[0, MOE_GROUPS:MOE_GROUPS + N_EXPERTS].astype(jnp.int32)
    padded = ((counts + EXPERT_TILE - 1) // EXPERT_TILE) * EXPERT_TILE
    ends = jnp.cumsum(padded)
    starts = ends - padded
    e = route[:, 0:EXPERT_TOPK].astype(jnp.int32)
    rank = route[:, 4:4 + EXPERT_TOPK].astype(jnp.int32)
    pos = starts[e] + rank
    pos_tiles = pos.reshape(t // tm, tm, EXPERT_TOPK).transpose(0, 2, 1).reshape(t // tm, EXPERT_TOPK * tm)
    n_tiles = (EXPERT_TOPK * t) // EXPERT_TILE + N_EXPERTS
    tile_start = jnp.arange(n_tiles, dtype=jnp.int32) * EXPERT_TILE
    tile_expert = jnp.minimum(jnp.searchsorted(ends, tile_start, side="right"), N_EXPERTS - 1).astype(jnp.int32)
    n_used = (ends[-1] // EXPERT_TILE).astype(jnp.int32).reshape(1)
    return pos_tiles, tile_expert, n_used, n_tiles * EXPERT_TILE


def _pack_w_in(w_in):
    o = np.cumsum([0, SSM_D_INNER, SSM_CONV_DIM, SSM_HEADS, NSA_WIDTH] + [NSA_KV_WIDTH] * 6
                  + [3 * NSA_HEADS, D_MODEL, D_MODEL])
    z_xbc = w_in[:, o[0]:o[2]]
    w_dt = w_in[:, o[2]:o[3]]
    w_q = w_in[:, o[3]:o[4]]
    w_kv = w_in[:, o[4]:o[10]]
    w_gn = w_in[:, o[10]:o[11]]
    w_gsa = w_in[:, o[11]:o[13]]
    w_main = jnp.concatenate([z_xbc, w_q, w_gsa, w_kv], axis=1).astype(bf16)
    w_small = jnp.concatenate(
        [w_dt, w_gn, jnp.zeros((w_in.shape[0], LANES - SSM_HEADS - 3 * NSA_HEADS), f32)], axis=1)
    return w_main, w_small


def _layer(h, ada3, cos_t, sin_t, batch, seq, norm1_g, w_in, conv_w, conv_b, dt_bias, a_log, d_skip, ssm_norm_g,
           w_ssm_out, cmp_pe_k, cmp_w1_k, cmp_w2_k, cmp_pe_v, cmp_w1_v, cmp_w2_v, w_nsa_out, w_o, norm2_g,
           w_router_group, b_router_group, w_router_expert, b_router_expert, w_exp_gate, w_exp_up, w_exp_down,
           final_g):
    t, d = h.shape
    w_main, w_small = _pack_w_in(w_in)
    hn, small = _hn(h, ada3, norm1_g, w_small, seq)
    main = _inproj(hn, w_main)
    yn = _ssd(main, small, conv_w, conv_b, dt_bias, a_log, d_skip, ssm_norm_g, batch, seq)
    o_nsa = _nsa(main, small, cos_t, sin_t, cmp_pe_k, cmp_w1_k, cmp_w2_k, cmp_pe_v, cmp_w1_v, cmp_w2_v, batch, seq)
    rpad = LANES - MOE_GROUPS - N_EXPERTS
    w_route = jnp.concatenate([w_router_group, w_router_expert, jnp.zeros((d, rpad), f32)], axis=1)
    b_route = jnp.concatenate([b_router_group, b_router_expert, jnp.zeros((rpad,), f32)]).reshape(1, LANES)
    h1, hn2, route, cnt = _post(yn, o_nsa, main, h, ada3, w_ssm_out.astype(bf16), w_nsa_out.astype(bf16),
                                w_o.astype(bf16), norm2_g, w_route, b_route, seq)
    tm = 256
    pos_tiles, tile_expert, n_used, n_rows = _routing_tables(route, cnt, t, tm)
    xs = _dispatch(pos_tiles, hn2, n_rows)
    ys = _experts(tile_expert, n_used, xs, w_exp_gate.astype(bf16), w_exp_up.astype(bf16), w_exp_down.astype(bf16))
    return _combine(pos_tiles, ys, h1, route, ada3, final_g, seq)


def kernel(x, c, positions, w_ada, b_ada, norm1_g, w_in, conv_w, conv_b, dt_bias, a_log, d_skip, ssm_norm_g, w_ssm_out, cmp_pe_k, cmp_w1_k, cmp_w2_k, cmp_pe_v, cmp_w1_v, cmp_w2_v, w_nsa_out, w_o, norm2_g, w_router_group, b_router_group, w_router_expert, b_router_expert, w_exp_gate, w_exp_up, w_exp_down, final_g):
    batch, seq, d = x.shape
    depth = w_ada.shape[0]
    assert depth == 1, "final RMSNorm is fused into the single layer's MoE combine"
    cos_t, sin_t = _rope_tables(positions)
    h = x.reshape(batch * seq, d)
    l = 0
    ada3 = _ada(c, w_ada[l], b_ada[l]).reshape(batch, 6, d)
    out = _layer(h, ada3, cos_t, sin_t, batch, seq, norm1_g[l], w_in[l], conv_w[l], conv_b[l], dt_bias[l],
                 a_log[l], d_skip[l], ssm_norm_g[l], w_ssm_out[l], cmp_pe_k[l], cmp_w1_k[l], cmp_w2_k[l],
                 cmp_pe_v[l], cmp_w1_v[l], cmp_w2_v[l], w_nsa_out[l], w_o[l], norm2_g[l], w_router_group[l],
                 b_router_group[l], w_router_expert[l], b_router_expert[l], w_exp_gate[l], w_exp_up[l],
                 w_exp_down[l], final_g)
    return out.reshape(batch, seq, d)
```

```python
import functools
import math

import numpy as np
import jax
import jax.numpy as jnp
from jax import lax
from jax.experimental import pallas as pl
from jax.experimental.pallas import tpu as pltpu

D_MODEL = 1024
SSM_D_INNER = 2048
SSM_HEAD_DIM = 64
SSM_HEADS = 32
SSM_GROUPS = 4
SSM_HPG = 8
SSM_STATE = 128
SSM_CONV = 4
SSM_CHUNK = 128
SSM_CONV_DIM = 3072
NSA_HEADS = 16
NSA_HEAD_DIM = 64
NSA_WIDTH = 1024
NSA_KV_GROUPS = 2
NSA_HPG = 8
NSA_KV_WIDTH = 128
CMP_BLOCK = 32
CMP_STRIDE = 16
CMP_HIDDEN = 256
SEL_BLOCK = 64
SEL_TOPK = 4
WINDOW = 512
NSA_QBLOCK = 64
ROPE_THETA = 10000.0
MOE_GROUPS = 4
EXPERTS_PER_GROUP = 8
N_EXPERTS = 32
EXPERT_TOPK = 2
EXPERT_HIDDEN = 256
NORM_EPS = 1e-6

LANES = 128
NEG = -1e30
NSA_QTILE = 128
WIN_KEYS = WINDOW + NSA_QTILE
WIN_PAD = WINDOW
KEY_CHUNK = 256
VMEM_LIMIT = 56 * 1024 * 1024

C_Z, C_X, C_BC, C_Q, C_GS, C_GA, C_KV, C_MAIN = 0, 2048, 4096, 5120, 6144, 7168, 8192, 8960

f32 = jnp.float32
bf16 = jnp.bfloat16


def _cparams(sem):
    return pltpu.CompilerParams(dimension_semantics=sem, vmem_limit_bytes=VMEM_LIMIT)


def _dot(a, b):
    return jnp.dot(a, b, preferred_element_type=f32)


def _dot_nt(a, b):
    return lax.dot_general(a, b, (((1,), (1,)), ((), ())), preferred_element_type=f32)


def _split3(x):
    hi = x.astype(bf16)
    r = x - hi.astype(f32)
    mid = r.astype(bf16)
    lo = (r - mid.astype(f32)).astype(bf16)
    return hi, mid, lo


def _dot_exact_rhs(a, b_bf16):
    hi, mid, lo = _split3(a)
    return _dot(hi, b_bf16) + _dot(mid, b_bf16) + _dot(lo, b_bf16)


def _dot_hp(a, b):
    ah = a.astype(bf16)
    al = (a - ah.astype(f32)).astype(bf16)
    bh = b.astype(bf16)
    bl = (b - bh.astype(f32)).astype(bf16)
    return _dot(ah, bh) + _dot(al, bh) + _dot(ah, bl)


def _silu(x):
    return x * jax.nn.sigmoid(x)


def _ada_kernel(c_ref, w_ref, b_ref, o_ref):
    o_ref[...] = _dot_hp(_silu(c_ref[...]), w_ref[...]) + b_ref[...]


def _ada(c, w_ada, b_ada):
    b, d = c.shape
    n = w_ada.shape[1]
    tn = 1024
    return pl.pallas_call(
        _ada_kernel,
        grid=(n // tn,),
        in_specs=[pl.BlockSpec((b, d), lambda j: (0, 0)),
                  pl.BlockSpec((d, tn), lambda j: (0, j)),
                  pl.BlockSpec((1, tn), lambda j: (0, j))],
        out_specs=pl.BlockSpec((b, tn), lambda j: (0, j)),
        out_shape=jax.ShapeDtypeStruct((b, n), f32),
        compiler_params=_cparams(("arbitrary",)),
    )(c, w_ada, b_ada.reshape(1, n))


def _rope_kernel(pos_ref, inv_ref, sgn_ref, cos_ref, sin_ref):
    ang = pos_ref[...].astype(f32) * inv_ref[...]
    cos_ref[...] = jnp.cos(ang)
    sin_ref[...] = jnp.sin(ang) * sgn_ref[...]


def _rope_tables(positions):
    t = positions.size
    tm = 2048
    inv = ROPE_THETA ** (-jnp.arange(0, NSA_HEAD_DIM, 2, dtype=f32) / NSA_HEAD_DIM)
    inv_full = jnp.tile(inv, LANES // 32).reshape(1, LANES)
    lane = np.arange(LANES)
    sgn = jnp.asarray(np.where((lane % 64) < 32, -1.0, 1.0), f32).reshape(1, LANES)
    return pl.pallas_call(
        _rope_kernel,
        grid=(t // tm,),
        in_specs=[pl.BlockSpec((tm, 1), lambda i: (i, 0)),
                  pl.BlockSpec((1, LANES), lambda i: (0, 0)),
                  pl.BlockSpec((1, LANES), lambda i: (0, 0))],
        out_specs=[pl.BlockSpec((tm, LANES), lambda i: (i, 0))] * 2,
        out_shape=[jax.ShapeDtypeStruct((t, LANES), f32)] * 2,
        compiler_params=_cparams(("arbitrary",)),
    )(positions.reshape(t, 1), inv_full, sgn)


def _hn_kernel(x_ref, ada_ref, g_ref, ws_ref, hn_ref, small_ref):
    x = x_ref[...]
    y = x * lax.rsqrt(jnp.mean(x * x, axis=-1, keepdims=True) + NORM_EPS) * g_ref[...]
    ada = ada_ref[0]
    hn = y * (1.0 + ada[1:2]) + ada[0:1]
    hn_ref[...] = hn.astype(bf16)
    small_ref[...] = _dot_hp(hn, ws_ref[...])


def _hn(x2, ada3, norm1_g, w_small, seq):
    t, d = x2.shape
    tm = 512
    return pl.pallas_call(
        _hn_kernel,
        grid=(t // tm,),
        in_specs=[pl.BlockSpec((tm, d), lambda i: (i, 0)),
                  pl.BlockSpec((1, 6, d), lambda i: ((i * tm) // seq, 0, 0)),
                  pl.BlockSpec((1, d), lambda i: (0, 0)),
                  pl.BlockSpec((d, LANES), lambda i: (0, 0))],
        out_specs=[pl.BlockSpec((tm, d), lambda i: (i, 0)),
                   pl.BlockSpec((tm, LANES), lambda i: (i, 0))],
        out_shape=[jax.ShapeDtypeStruct((t, d), bf16),
                   jax.ShapeDtypeStruct((t, LANES), f32)],
        compiler_params=_cparams(("arbitrary",)),
    )(x2, ada3, norm1_g.reshape(1, d), w_small)


def _mm_kernel(a_ref, b_ref, o_ref):
    o_ref[...] = _dot(a_ref[...], b_ref[...]).astype(o_ref.dtype)


def _inproj(hn, w_main):
    t, d = hn.shape
    n = w_main.shape[1]
    tm = 1024 if t % 1024 == 0 else t
    tn = 1792
    return pl.pallas_call(
        _mm_kernel,
        grid=(t // tm, n // tn),
        in_specs=[pl.BlockSpec((tm, d), lambda i, j: (i, 0)),
                  pl.BlockSpec((d, tn), lambda i, j: (0, j))],
        out_specs=pl.BlockSpec((tm, tn), lambda i, j: (i, j)),
        out_shape=jax.ShapeDtypeStruct((t, n), bf16),
        compiler_params=_cparams(("arbitrary", "arbitrary")),
    )(hn, w_main)


def _softplus(x):
    return jnp.maximum(x, 0.0) + jnp.log1p(jnp.exp(-jnp.abs(x)))


def _ssd_kernel(z_ref, xs_ref, bc_ref, sm_ref, cw_ref, cb_ref, dtb_ref, alog_ref, dsk_ref, ng_ref, e_ref,
                o_ref, xbuf, bcbuf, state, ybuf):
    q = SSM_CHUNK
    n_st = SSM_STATE
    gw = SSM_HPG * SSM_HEAD_DIM

    @pl.when(pl.program_id(1) == 0)
    def _():
        xbuf[0:8, :] = jnp.zeros((8, SSM_D_INNER), f32)
        bcbuf[0:8, :] = jnp.zeros((8, 2 * SSM_GROUPS * n_st), f32)
        state[...] = jnp.zeros(state.shape, f32)

    xbuf[8:8 + q, :] = xs_ref[...].astype(f32)
    bcbuf[8:8 + q, :] = bc_ref[...].astype(f32)

    def conv(buf, lo, hi):
        acc = cb_ref[:, lo:hi] + cw_ref[0:1, lo:hi] * buf[pl.ds(5, q), :]
        for k in range(1, SSM_CONV):
            acc = acc + cw_ref[k:k + 1, lo:hi] * buf[pl.ds(5 + k, q), :]
        return _silu(acc)

    xc = conv(xbuf, 0, SSM_D_INNER)
    bcm = conv(bcbuf, SSM_D_INNER, SSM_CONV_DIM)
    xbuf[0:8, :] = xbuf[q:q + 8, :]
    bcbuf[0:8, :] = bcbuf[q:q + 8, :]

    lane = lax.broadcasted_iota(jnp.int32, (1, LANES), 1)
    dt = _softplus(sm_ref[...] + dtb_ref[...])
    a = jnp.where(lane < SSM_HEADS, -jnp.exp(alog_ref[...]), 0.0)
    da = dt * a
    ri = lax.broadcasted_iota(jnp.int32, (q, q), 0)
    ci = lax.broadcasted_iota(jnp.int32, (q, q), 1)
    tril = ci <= ri
    tril_b = jnp.where(tril, 1.0, 0.0).astype(bf16)
    acum = _dot_exact_rhs_left(tril_b, da)
    acum_t = acum.T
    a_last = acum[q - 1:q, :]
    exp_a = jnp.exp(acum)
    w_end = jnp.exp(a_last - acum) * dt
    e_mat = e_ref[...]
    dt_x = _dot(dt.astype(bf16), e_mat)
    expa_x = _dot(exp_a.astype(bf16), e_mat)
    wend_x = _dot(w_end.astype(bf16), e_mat)
    elast_x = _dot_exact_rhs(jnp.broadcast_to(jnp.exp(a_last), (8, LANES)), e_mat)[0:1, :]
    xdt = (xc * dt_x).astype(bf16)
    xw = (xc * wend_x).astype(bf16)
    lo_mask = lane < SSM_HEAD_DIM

    for g in range(SSM_GROUPS):
        bg = bcm[:, g * n_st:(g + 1) * n_st]
        cg = bcm[:, SSM_GROUPS * n_st + g * n_st:SSM_GROUPS * n_st + (g + 1) * n_st]
        bg_b = bg.astype(bf16)
        cg_b = cg.astype(bf16)
        cb = _dot_nt(cg_b, bg_b)
        st = state[g]
        yoff = _dot(cg_b, st.astype(bf16)) * expa_x[:, g * gw:(g + 1) * gw]
        for hp in range(SSM_HPG // 2):
            pair = g * (SSM_HPG // 2) + hp
            xpair = xdt[:, pair * LANES:(pair + 1) * LANES]
            acc = yoff[:, hp * LANES:(hp + 1) * LANES]
            for par in range(2):
                h = 2 * pair + par
                xx = jnp.where(lo_mask, xpair, 0.0) if par == 0 else jnp.where(lo_mask, 0.0, xpair)
                seg = acum[:, h:h + 1] - acum_t[h:h + 1, :]
                decay = jnp.exp(jnp.where(tril, seg, NEG))
                acc = acc + _dot((cb * decay).astype(bf16), xx.astype(bf16))
            ybuf[:, pair * LANES:(pair + 1) * LANES] = acc
        state[g] = st * elast_x[:, g * gw:(g + 1) * gw] + _dot(bg.T.astype(bf16), xw[:, g * gw:(g + 1) * gw])

    z = z_ref[...].astype(f32)
    y = (ybuf[...] + dsk_ref[...] * xc) * _silu(z)
    for g in range(SSM_GROUPS):
        yg = y[:, g * gw:(g + 1) * gw]
        ms = jnp.mean(yg * yg, axis=-1, keepdims=True)
        o_ref[:, g * gw:(g + 1) * gw] = (yg * lax.rsqrt(ms + NORM_EPS) * ng_ref[:, g * gw:(g + 1) * gw]).astype(bf16)


def _dot_exact_rhs_left(l_bf16, x):
    hi, mid, lo = _split3(x)
    return _dot(l_bf16, hi) + _dot(l_bf16, mid) + _dot(l_bf16, lo)


def _ssd(main, small, conv_w, conv_b, dt_bias, a_log, d_skip, ssm_norm_g, batch, seq):
    t = main.shape[0]
    q = SSM_CHUNK
    nc = seq // q
    pad = LANES - SSM_HEADS
    dtb = jnp.pad(dt_bias, (0, pad)).reshape(1, LANES)
    alog = jnp.pad(a_log, (0, pad)).reshape(1, LANES)
    dsk = jnp.repeat(d_skip, SSM_HEAD_DIM).reshape(1, SSM_D_INNER)
    e_np = (np.arange(LANES)[:, None] == (np.arange(SSM_D_INNER)[None, :] // SSM_HEAD_DIM)).astype(np.float32)
    e_mat = jnp.asarray(e_np, bf16)
    row = lambda b, c: (b * nc + c)
    const = lambda b, c: (0, 0)
    return pl.pallas_call(
        _ssd_kernel,
        grid=(batch, nc),
        in_specs=[pl.BlockSpec((q, SSM_D_INNER), lambda b, c: (row(b, c), C_Z // SSM_D_INNER)),
                  pl.BlockSpec((q, SSM_D_INNER), lambda b, c: (row(b, c), C_X // SSM_D_INNER)),
                  pl.BlockSpec((q, 1024), lambda b, c: (row(b, c), C_BC // 1024)),
                  pl.BlockSpec((q, LANES), lambda b, c: (row(b, c), 0)),
                  pl.BlockSpec((SSM_CONV, SSM_CONV_DIM), const),
                  pl.BlockSpec((1, SSM_CONV_DIM), const),
                  pl.BlockSpec((1, LANES), const),
                  pl.BlockSpec((1, LANES), const),
                  pl.BlockSpec((1, SSM_D_INNER), const),
                  pl.BlockSpec((1, SSM_D_INNER), const),
                  pl.BlockSpec((LANES, SSM_D_INNER), const)],
        out_specs=pl.BlockSpec((q, SSM_D_INNER), lambda b, c: (row(b, c), 0)),
        out_shape=jax.ShapeDtypeStruct((t, SSM_D_INNER), bf16),
        scratch_shapes=[pltpu.VMEM((q + 8, SSM_D_INNER), f32),
                        pltpu.VMEM((q + 8, 2 * SSM_GROUPS * SSM_STATE), f32),
                        pltpu.VMEM((SSM_GROUPS, SSM_STATE, SSM_HPG * SSM_HEAD_DIM), f32),
                        pltpu.VMEM((q, SSM_D_INNER), f32)],
        compiler_params=_cparams(("arbitrary", "arbitrary")),
    )(main, main, main, small, conv_w, conv_b.reshape(1, -1), dtb, alog, dsk,
      ssm_norm_g.reshape(1, -1), e_mat)


def _nsa_kernel(q_ref, kc_ref, vc_ref, ks_ref, vs_ref, kw_ref, vw_ref, sm_ref, cos_ref, sin_ref,
                pek_ref, w1k_ref, w2k_ref, pev_ref, w1v_ref, w2v_ref, ovl_ref, eg_ref,
                o_ref,
                ks_p, vs_p, kw_p, vw_p, kc_buf, vc_buf, kcmp_p, vcmp_p, q_buf, s_buf, b_buf, mx_buf, l_buf, o_acc):
    g = pl.program_id(1)
    seq = q_ref.shape[0]
    qb = NSA_QTILE
    dk = NSA_HEAD_DIM
    n_cmp = seq // CMP_STRIDE
    n_pairs = NSA_HPG // 2
    rows = NSA_HPG * qb
    n_key_chunks = seq // KEY_CHUNK
    sel_shift = SEL_BLOCK.bit_length() - 1
    scale = dk ** -0.5

    lane = lax.broadcasted_iota(jnp.int32, (1, LANES), 1)
    lo = lane < dk
    first_half = (lane & (dk - 1)) < dk // 2

    def rope(tv, cs, sn):
        rot = jnp.where(first_half, pltpu.roll(tv, LANES - dk // 2, 1), pltpu.roll(tv, dk // 2, 1))
        return tv * cs + rot * sn

    def group_lanes(tv):
        return jnp.where(lo, jnp.where(g == 0, tv, pltpu.roll(tv, dk, 1)), 0.0)

    prep_rows = 256

    def prep(i, carry):
        r0 = pl.multiple_of(i * prep_rows, prep_rows)
        sl = pl.ds(r0, prep_rows)
        cs = cos_ref[sl, :]
        sn = sin_ref[sl, :]
        ks_p[sl, :] = group_lanes(rope(ks_ref[sl, :].astype(f32), cs, sn)).astype(bf16)
        vs_p[sl, :] = group_lanes(vs_ref[sl, :].astype(f32)).astype(bf16)
        slw = pl.ds(pl.multiple_of(r0 + WIN_PAD, 64), prep_rows)
        kw_p[slw, :] = group_lanes(rope(kw_ref[sl, :].astype(f32), cs, sn)).astype(bf16)
        vw_p[slw, :] = group_lanes(vw_ref[sl, :].astype(f32)).astype(bf16)
        kc_buf[sl, :] = group_lanes(rope(kc_ref[sl, :].astype(f32), cs, sn))
        vc_buf[sl, :] = group_lanes(vc_ref[sl, :].astype(f32))
        return carry

    lax.fori_loop(0, seq // prep_rows, prep, 0)
    kw_p[0:WIN_PAD, :] = jnp.zeros((WIN_PAD, LANES), bf16)
    vw_p[0:WIN_PAD, :] = jnp.zeros((WIN_PAD, LANES), bf16)
    kc_buf[seq:seq + CMP_BLOCK, :] = jnp.zeros((CMP_BLOCK, LANES), f32)
    vc_buf[seq:seq + CMP_BLOCK, :] = jnp.zeros((CMP_BLOCK, LANES), f32)

    def compress(buf, pe_ref, w1_ref, w2_ref, out):
        pre = jnp.zeros((n_cmp, CMP_HIDDEN), f32)
        for l in range(CMP_BLOCK):
            xl = buf[pl.ds(l, n_cmp, stride=CMP_STRIDE), :] + pe_ref[l:l + 1, :]
            pre = pre + _dot(xl.astype(bf16), w1_ref[l * LANES:(l + 1) * LANES, :])
        out[...] = _dot(_silu(pre).astype(bf16), w2_ref[...]).astype(bf16)

    compress(kc_buf, pek_ref, w1k_ref, w2k_ref, kcmp_p)
    compress(vc_buf, pev_ref, w1v_ref, w2v_ref, vcmp_p)

    def tile8(x):
        return jnp.concatenate([x] * NSA_HPG, axis=0)

    wr = lax.broadcasted_iota(jnp.int32, (qb, WIN_KEYS), 0)
    wc = lax.broadcasted_iota(jnp.int32, (qb, WIN_KEYS), 1)

    def qblock(qi, carry):
        s0 = pl.multiple_of(qi * qb, qb)
        sl = pl.ds(s0, qb)
        tq = s0 + lax.broadcasted_iota(jnp.int32, (qb, 1), 0)

        q2 = q_ref[sl, :].astype(f32)
        qs = jnp.concatenate([q2[:, p * LANES:(p + 1) * LANES] for p in range(n_pairs)], axis=0)
        cs = jnp.concatenate([cos_ref[sl, :]] * n_pairs, axis=0)
        sn = jnp.concatenate([sin_ref[sl, :]] * n_pairs, axis=0)
        qr = rope(qs, cs, sn) * scale
        q_even = jnp.where(lo, qr, 0.0)
        q_odd = jnp.where(lo, pltpu.roll(qr, dk, 1), 0.0)
        qall = jnp.concatenate([q_even, q_odd], axis=0).astype(bf16)

        sc = _dot_nt(qall, kcmp_p[...])
        ccol = lax.broadcasted_iota(jnp.int32, (qb, n_cmp), 1)
        cmask = (ccol * CMP_STRIDE + (CMP_BLOCK - 1)) <= tq
        cbias = tile8(jnp.where(cmask, 0.0, NEG))
        cmf = tile8(jnp.where(cmask, 1.0, 0.0))
        scb = sc + cbias
        pc = jnp.exp(scb - jnp.max(scb, axis=-1, keepdims=True)) * cmf
        pc = pc / jnp.maximum(jnp.sum(pc, axis=-1, keepdims=True), 1e-30)
        o_cmp = _dot(pc.astype(bf16), vcmp_p[...])
        pcs = pc[0:qb]
        for r in range(1, NSA_HPG):
            pcs = pcs + pc[r * qb:(r + 1) * qb]
        imp = _dot_exact_rhs(pcs, ovl_ref[...])

        jl = lax.broadcasted_iota(jnp.int32, (qb, LANES), 1)
        local = tq >> sel_shift
        prio = jnp.where(jl == local, 1e6, jnp.where(jl > local, -jnp.inf, jnp.where(jl == 0, 5e5, 0.0)))
        val = imp + prio
        sel = jnp.zeros((qb, LANES), f32)
        for _ in range(SEL_TOPK):
            mx = jnp.max(val, axis=-1, keepdims=True)
            idx = jnp.min(jnp.where(val == mx, jl, LANES), axis=-1, keepdims=True)
            hit = jl == idx
            sel = jnp.where(hit, 1.0, sel)
            val = jnp.where(hit, -jnp.inf, val)
        sel_b = sel.astype(bf16)

        q_buf[...] = qall
        o_acc[0] = o_cmp

        kk = kw_p[pl.ds(s0, WIN_KEYS), :]
        vv = vw_p[pl.ds(s0, WIN_KEYS), :]
        rel = wc - WIN_PAD - wr
        ok = (rel <= 0) & (rel > -WINDOW) & ((wc - WIN_PAD + s0) >= 0)
        wbias = jnp.concatenate([jnp.where(ok, 0.0, NEG)] * 2, axis=0)
        half = n_pairs * qb
        for hr in range(rows // (2 * qb)):
            rs = slice(hr * 2 * qb, (hr + 1) * 2 * qb)
            sw = _dot_nt(qall[rs], kk) + wbias
            pw = jnp.exp(sw - jnp.max(sw, axis=-1, keepdims=True))
            o_acc[2, rs, :] = _dot(pw.astype(bf16), vv) / jnp.sum(pw, axis=-1, keepdims=True)

        er = lax.broadcasted_iota(jnp.int32, (LANES, KEY_CHUNK), 0)
        ec = lax.broadcasted_iota(jnp.int32, (LANES, KEY_CHUNK), 1)
        kr = lax.broadcasted_iota(jnp.int32, (qb, KEY_CHUNK), 1)
        for c in range(n_key_chunks):
            expand = jnp.where(er == c * (KEY_CHUNK // SEL_BLOCK) + (ec >> sel_shift), 1.0, 0.0).astype(bf16)
            ok_s = (_dot(sel_b, expand) > 0.5) & ((c * KEY_CHUNK + kr) <= tq)
            b_buf[c] = jnp.where(ok_s, 0.0, NEG)
        n_chunks = ((s0 + qb - 1) >> (KEY_CHUNK.bit_length() - 1)) + 1
        n_steps = (n_chunks + 1) >> 1
        mx_buf[...] = jnp.full((rows, LANES), NEG, f32)

        def pass_a(step, carry_a):
            mx = mx_buf[...]
            for u in range(2):
                c = 2 * step + u
                k0 = pl.multiple_of(c * KEY_CHUNK, KEY_CHUNK)
                s = _dot_nt(q_buf[...], ks_p[pl.ds(k0, KEY_CHUNK), :]) + tile8(b_buf[c])
                s_buf[c] = s
                mx = jnp.maximum(mx, jnp.maximum(s[:, :LANES], s[:, LANES:]))
            mx_buf[...] = mx
            return carry_a

        lax.fori_loop(0, n_steps, pass_a, 0)
        mx_buf[...] = jnp.broadcast_to(jnp.max(mx_buf[...], axis=-1, keepdims=True), (rows, LANES))
        l_buf[...] = jnp.zeros((rows, LANES), f32)
        o_acc[1] = jnp.zeros((rows, LANES), f32)

        def pass_b(step, carry_b):
            mb = mx_buf[...]
            lsum = l_buf[...]
            acc = o_acc[1]
            for u in range(2):
                c = 2 * step + u
                k0 = pl.multiple_of(c * KEY_CHUNK, KEY_CHUNK)
                s = s_buf[c]
                p0 = jnp.exp(s[:, :LANES] - mb)
                p1 = jnp.exp(s[:, LANES:] - mb)
                lsum = lsum + (p0 + p1)
                acc = acc + _dot(jnp.concatenate([p0, p1], axis=1).astype(bf16), vs_p[pl.ds(k0, KEY_CHUNK), :])
            l_buf[...] = lsum
            o_acc[1] = acc
            return carry_b

        lax.fori_loop(0, n_steps, pass_b, 0)
        o_cmp = o_acc[0]
        o_slc = o_acc[1] / jnp.sum(l_buf[...], axis=-1, keepdims=True)
        o_win = o_acc[2]

        gates = _dot(jax.nn.sigmoid(sm_ref[sl, :]).astype(bf16), eg_ref[0])

        def pair_tile(o, p):
            return o[p * qb:(p + 1) * qb] + pltpu.roll(o[half + p * qb:half + (p + 1) * qb], dk, 1)

        width = NSA_HPG * dk
        for p in range(n_pairs):
            cols = slice(p * LANES, (p + 1) * LANES)
            out = (gates[:, 0 * width + p * LANES:0 * width + (p + 1) * LANES] * pair_tile(o_cmp, p)
                   + gates[:, 1 * width + p * LANES:1 * width + (p + 1) * LANES] * pair_tile(o_slc, p)
                   + gates[:, 2 * width + p * LANES:2 * width + (p + 1) * LANES] * pair_tile(o_win, p))
            o_ref[sl, cols] = out.astype(bf16)
        return carry

    lax.fori_loop(0, seq // qb, qblock, 0)


def _pad_w1(w1):
    h = w1.shape[1]
    w = w1.reshape(CMP_BLOCK, NSA_HEAD_DIM, h)
    return jnp.pad(w, ((0, 0), (0, LANES - NSA_HEAD_DIM), (0, 0))).reshape(CMP_BLOCK * LANES, h).astype(bf16)


def _nsa(main, small, cos_t, sin_t, cmp_pe_k, cmp_w1_k, cmp_w2_k, cmp_pe_v, cmp_w1_v, cmp_w2_v, batch, seq):
    t = main.shape[0]
    assert seq % (2 * KEY_CHUNK) == 0 and seq // SEL_BLOCK <= LANES
    dk = NSA_HEAD_DIM
    width = NSA_HPG * dk
    n_cmp = seq // CMP_STRIDE
    pad_l = ((0, 0), (0, LANES - dk))
    pek = jnp.pad(cmp_pe_k, pad_l)
    pev = jnp.pad(cmp_pe_v, pad_l)
    w1k = _pad_w1(cmp_w1_k)
    w1v = _pad_w1(cmp_w1_v)
    w2k = jnp.pad(cmp_w2_k, pad_l).astype(bf16)
    w2v = jnp.pad(cmp_w2_v, pad_l).astype(bf16)
    cstart = np.arange(LANES)[:, None] * CMP_STRIDE
    sstart = np.arange(LANES)[None, :] * SEL_BLOCK
    valid_c = np.arange(LANES)[:, None] < (seq - CMP_BLOCK) // CMP_STRIDE + 1
    valid_j = np.arange(LANES)[None, :] < seq // SEL_BLOCK
    ovl = ((cstart < sstart + SEL_BLOCK) & (cstart + CMP_BLOCK > sstart) & valid_c & valid_j).astype(np.float32)
    ovl = jnp.asarray(ovl[:n_cmp] if n_cmp <= LANES else ovl, bf16)
    eg = np.zeros((NSA_KV_GROUPS, LANES, 3 * width), np.float32)
    for gg in range(NSA_KV_GROUPS):
        for r in range(NSA_HPG):
            for j in range(3):
                eg[gg, SSM_HEADS + gg * NSA_HPG * 3 + r * 3 + j, j * width + r * dk:j * width + (r + 1) * dk] = 1.0
    eg = jnp.asarray(eg, bf16)

    kv_blk = C_KV // LANES
    kv_spec = lambda j: pl.BlockSpec((seq, LANES), lambda b, g: (b, kv_blk + j))
    const2 = lambda b, g: (0, 0)
    return pl.pallas_call(
        _nsa_kernel,
        grid=(batch, NSA_KV_GROUPS),
        in_specs=[pl.BlockSpec((seq, width), lambda b, g: (b, C_Q // width + g))]
                 + [kv_spec(j) for j in range(6)]
                 + [pl.BlockSpec((seq, LANES), lambda b, g: (b, 0))] * 3
                 + [pl.BlockSpec((CMP_BLOCK, LANES), const2),
                    pl.BlockSpec((CMP_BLOCK * LANES, CMP_HIDDEN), const2),
                    pl.BlockSpec((CMP_HIDDEN, LANES), const2)] * 2
                 + [pl.BlockSpec(ovl.shape, const2),
                    pl.BlockSpec((1, LANES, 3 * width), lambda b, g: (g, 0, 0))],
        out_specs=pl.BlockSpec((seq, width), lambda b, g: (b, g)),
        out_shape=jax.ShapeDtypeStruct((t, NSA_WIDTH), bf16),
        scratch_shapes=[pltpu.VMEM((seq, LANES), bf16),
                        pltpu.VMEM((seq, LANES), bf16),
                        pltpu.VMEM((seq + WIN_PAD, LANES), bf16),
                        pltpu.VMEM((seq + WIN_PAD, LANES), bf16),
                        pltpu.VMEM((seq + CMP_BLOCK, LANES), f32),
                        pltpu.VMEM((seq + CMP_BLOCK, LANES), f32),
                        pltpu.VMEM((n_cmp, LANES), bf16),
                        pltpu.VMEM((n_cmp, LANES), bf16),
                        pltpu.VMEM((NSA_HPG * NSA_QTILE, LANES), bf16),
                        pltpu.VMEM((seq // KEY_CHUNK, NSA_HPG * NSA_QTILE, KEY_CHUNK), f32),
                        pltpu.VMEM((seq // KEY_CHUNK, NSA_QTILE, KEY_CHUNK), f32),
                        pltpu.VMEM((NSA_HPG * NSA_QTILE, LANES), f32),
                        pltpu.VMEM((NSA_HPG * NSA_QTILE, LANES), f32),
                        pltpu.VMEM((3, NSA_HPG * NSA_QTILE, LANES), f32)],
        compiler_params=_cparams(("arbitrary", "arbitrary")),
    )(main, main, main, main, main, main, main, small, cos_t, sin_t,
      pek, w1k, w2k, pev, w1v, w2v, ovl, eg)


def _post_kernel(yn_ref, on_ref, gs_ref, ga_ref, x_ref, ada_ref, wso_ref, wno_ref, wo_ref, g2_ref, wr_ref, br_ref,
                 h1_ref, hn2_ref, cw_ref):
    tm = x_ref.shape[0]
    y_ssm = _dot(yn_ref[...], wso_ref[...])
    y_nsa = _dot(on_ref[...], wno_ref[...])
    merged = jax.nn.sigmoid(gs_ref[...].astype(f32)) * y_ssm + jax.nn.sigmoid(ga_ref[...].astype(f32)) * y_nsa
    ada = ada_ref[0]
    h1 = x_ref[...] + ada[2:3] * _dot(merged.astype(bf16), wo_ref[...])
    h1_ref[...] = h1
    hn2 = h1 * lax.rsqrt(jnp.mean(h1 * h1, axis=-1, keepdims=True) + NORM_EPS) * g2_ref[...]
    hn2 = hn2 * (1.0 + ada[4:5]) + ada[3:4]
    hn2_ref[...] = hn2.astype(bf16)

    logits = _dot_hp(hn2, wr_ref[...]) + br_ref[...]
    lane = lax.broadcasted_iota(jnp.int32, (tm, LANES), 1)
    is_g = lane < MOE_GROUPS
    lg = jnp.where(is_g, logits, NEG)
    eg = jnp.where(is_g, jnp.exp(lg - jnp.max(lg, axis=-1, keepdims=True)), 0.0)
    pg_all = eg / jnp.sum(eg, axis=-1, keepdims=True)
    pg = jnp.max(pg_all, axis=-1, keepdims=True)
    gsel = jnp.min(jnp.where(is_g & (pg_all == pg), lane, LANES), axis=-1, keepdims=True)
    member = (lane >= MOE_GROUPS) & (lane < MOE_GROUPS + N_EXPERTS) & (((lane - MOE_GROUPS) >> 3) == gsel)
    le = jnp.where(member, logits, NEG)
    ee = jnp.where(member, jnp.exp(le - jnp.max(le, axis=-1, keepdims=True)), 0.0)
    pe = jnp.where(member, ee / jnp.sum(ee, axis=-1, keepdims=True), -1.0)
    v0 = jnp.max(pe, axis=-1, keepdims=True)
    i0 = jnp.min(jnp.where(pe == v0, lane, LANES), axis=-1, keepdims=True)
    hit0 = lane == i0
    pe1 = jnp.where(hit0, -1.0, pe)
    v1 = jnp.max(pe1, axis=-1, keepdims=True)
    i1 = jnp.min(jnp.where(pe1 == v1, lane, LANES), axis=-1, keepdims=True)
    hit1 = lane == i1
    den = v0 + v1
    w0 = pg * v0 / den
    w1 = pg * v1 / den

    cw_ref[...] = jnp.where(hit0, w0, jnp.where(hit1, w1, 0.0))


def _post(yn, o_nsa, main, x2, ada3, w_ssm_out, w_nsa_out, w_o, norm2_g, w_route, b_route, seq):
    t, d = x2.shape
    tm = 512
    const = lambda i: (0, 0)
    return pl.pallas_call(
        _post_kernel,
        grid=(t // tm,),
        in_specs=[pl.BlockSpec((tm, SSM_D_INNER), lambda i: (i, 0)),
                  pl.BlockSpec((tm, NSA_WIDTH), lambda i: (i, 0)),
                  pl.BlockSpec((tm, d), lambda i: (i, C_GS // d)),
                  pl.BlockSpec((tm, d), lambda i: (i, C_GA // d)),
                  pl.BlockSpec((tm, d), lambda i: (i, 0)),
                  pl.BlockSpec((1, 6, d), lambda i: ((i * tm) // seq, 0, 0)),
                  pl.BlockSpec((SSM_D_INNER, d), const),
                  pl.BlockSpec((NSA_WIDTH, d), const),
                  pl.BlockSpec((d, d), const),
                  pl.BlockSpec((1, d), const),
                  pl.BlockSpec((d, LANES), const),
                  pl.BlockSpec((1, LANES), const)],
        out_specs=[pl.BlockSpec((tm, d), lambda i: (i, 0)),
                   pl.BlockSpec((tm, d), lambda i: (i, 0)),
                   pl.BlockSpec((tm, LANES), lambda i: (i, 0))],
        out_shape=[jax.ShapeDtypeStruct((t, d), f32),
                   jax.ShapeDtypeStruct((t, d), bf16),
                   jax.ShapeDtypeStruct((t, LANES), f32)],
        compiler_params=_cparams(("arbitrary",)),
    )(yn, o_nsa, main, main, x2, ada3, w_ssm_out, w_nsa_out, w_o, norm2_g.reshape(1, d), w_route, b_route)


def _moe_kernel(h1_ref, x_ref, cw_ref, ada_ref, wgu_ref, wd_ref, fg_ref, o_ref, y_acc, hcat):
    g = pl.program_id(1)
    tm = x_ref.shape[0]
    hd = EXPERT_HIDDEN

    @pl.when(g == 0)
    def _():
        y_acc[...] = jnp.zeros(y_acc.shape, f32)

    x = x_ref[...]
    cw = cw_ref[...]
    lane = lax.broadcasted_iota(jnp.int32, (tm, LANES), 1)
    for e in range(EXPERTS_PER_GROUP):
        hgu = _dot(x, wgu_ref[0, :, 2 * e * hd:2 * (e + 1) * hd])
        ce = jnp.sum(jnp.where(lane == MOE_GROUPS + g * EXPERTS_PER_GROUP + e, cw, 0.0), axis=-1, keepdims=True)
        hcat[:, e * hd:(e + 1) * hd] = (_silu(hgu[:, :hd]) * hgu[:, hd:] * ce).astype(bf16)
    y_acc[...] += _dot(hcat[...], wd_ref[0])

    @pl.when(g == MOE_GROUPS - 1)
    def _():
        h2 = h1_ref[...] + ada_ref[0][5:6] * y_acc[...]
        o_ref[...] = h2 * lax.rsqrt(jnp.mean(h2 * h2, axis=-1, keepdims=True) + NORM_EPS) * fg_ref[...]


def _moe(h1, hn2, cw, ada3, w_gate, w_up, w_down, final_g, seq):
    t, d = h1.shape
    tm = 512
    hd = EXPERT_HIDDEN
    epg = EXPERTS_PER_GROUP
    wgu = jnp.concatenate([w_gate, w_up], axis=-1).astype(bf16).reshape(MOE_GROUPS, epg, d, 2 * hd)
    wgu = wgu.transpose(0, 2, 1, 3).reshape(MOE_GROUPS, d, epg * 2 * hd)
    wd = w_down.astype(bf16).reshape(MOE_GROUPS, epg * hd, d)
    return pl.pallas_call(
        _moe_kernel,
        grid=(t // tm, MOE_GROUPS),
        in_specs=[pl.BlockSpec((tm, d), lambda i, g: (i, 0)),
                  pl.BlockSpec((tm, d), lambda i, g: (i, 0)),
                  pl.BlockSpec((tm, LANES), lambda i, g: (i, 0)),
                  pl.BlockSpec((1, 6, d), lambda i, g: ((i * tm) // seq, 0, 0)),
                  pl.BlockSpec((1, d, epg * 2 * hd), lambda i, g: (g, 0, 0)),
                  pl.BlockSpec((1, epg * hd, d), lambda i, g: (g, 0, 0)),
                  pl.BlockSpec((1, d), lambda i, g: (0, 0))],
        out_specs=pl.BlockSpec((tm, d), lambda i, g: (i, 0)),
        out_shape=jax.ShapeDtypeStruct((t, d), f32),
        scratch_shapes=[pltpu.VMEM((tm, d), f32),
                        pltpu.VMEM((tm, epg * hd), bf16)],
        compiler_params=_cparams(("arbitrary", "arbitrary")),
    )(h1, hn2, cw, ada3, wgu, wd, final_g.reshape(1, d))


def _pack_w_in(w_in):
    o = np.cumsum([0, SSM_D_INNER, SSM_CONV_DIM, SSM_HEADS, NSA_WIDTH] + [NSA_KV_WIDTH] * 6
                  + [3 * NSA_HEADS, D_MODEL, D_MODEL])
    z_xbc = w_in[:, o[0]:o[2]]
    w_dt = w_in[:, o[2]:o[3]]
    w_q = w_in[:, o[3]:o[4]]
    w_kv = w_in[:, o[4]:o[10]]
    w_gn = w_in[:, o[10]:o[11]]
    w_gsa = w_in[:, o[11]:o[13]]
    w_main = jnp.concatenate([z_xbc, w_q, w_gsa, w_kv], axis=1).astype(bf16)
    w_small = jnp.concatenate(
        [w_dt, w_gn, jnp.zeros((w_in.shape[0], LANES - SSM_HEADS - 3 * NSA_HEADS), f32)], axis=1)
    return w_main, w_small


def _layer(h, ada3, cos_t, sin_t, batch, seq, norm1_g, w_in, conv_w, conv_b, dt_bias, a_log, d_skip, ssm_norm_g,
           w_ssm_out, cmp_pe_k, cmp_w1_k, cmp_w2_k, cmp_pe_v, cmp_w1_v, cmp_w2_v, w_nsa_out, w_o, norm2_g,
           w_router_group, b_router_group, w_router_expert, b_router_expert, w_exp_gate, w_exp_up, w_exp_down,
           final_g):
    t, d = h.shape
    w_main, w_small = _pack_w_in(w_in)
    hn, small = _hn(h, ada3, norm1_g, w_small, seq)
    main = _inproj(hn, w_main)
    yn = _ssd(main, small, conv_w, conv_b, dt_bias, a_log, d_skip, ssm_norm_g, batch, seq)
    o_nsa = _nsa(main, small, cos_t, sin_t, cmp_pe_k, cmp_w1_k, cmp_w2_k, cmp_pe_v, cmp_w1_v, cmp_w2_v, batch, seq)
    rpad = LANES - MOE_GROUPS - N_EXPERTS
    w_route = jnp.concatenate([w_router_group, w_router_expert, jnp.zeros((d, rpad), f32)], axis=1)
    b_route = jnp.concatenate([b_router_group, b_router_expert, jnp.zeros((rpad,), f32)]).reshape(1, LANES)
    h1, hn2, cw = _post(yn, o_nsa, main, h, ada3, w_ssm_out.astype(bf16), w_nsa_out.astype(bf16),
                        w_o.astype(bf16), norm2_g, w_route, b_route, seq)
    return _moe(h1, hn2, cw, ada3, w_exp_gate, w_exp_up, w_exp_down, final_g, seq)


def kernel(x, c, positions, w_ada, b_ada, norm1_g, w_in, conv_w, conv_b, dt_bias, a_log, d_skip, ssm_norm_g, w_ssm_out, cmp_pe_k, cmp_w1_k, cmp_w2_k, cmp_pe_v, cmp_w1_v, cmp_w2_v, w_nsa_out, w_o, norm2_g, w_router_group, b_router_group, w_router_expert, b_router_expert, w_exp_gate, w_exp_up, w_exp_down, final_g):
    batch, seq, d = x.shape
    depth = w_ada.shape[0]
    assert depth == 1, "final RMSNorm is fused into the single layer's MoE combine"
    cos_t, sin_t = _rope_tables(positions)
    h = x.reshape(batch * seq, d)
    l = 0
    ada3 = _ada(c, w_ada[l], b_ada[l]).reshape(batch, 6, d)
    out = _layer(h, ada3, cos_t, sin_t, batch, seq, norm1_g[l], w_in[l], conv_w[l], conv_b[l], dt_bias[l],
                 a_log[l], d_skip[l], ssm_norm_g[l], w_ssm_out[l], cmp_pe_k[l], cmp_w1_k[l], cmp_w2_k[l],
                 cmp_pe_v[l], cmp_w1_v[l], cmp_w2_v[l], w_nsa_out[l], w_o[l], norm2_g[l], w_router_group[l],
                 b_router_group[l], w_router_expert[l], b_router_expert[l], w_exp_gate[l], w_exp_up[l],
                 w_exp_down[l], final_g)
    return out.reshape(batch, seq, d)
```

```python
import functools
import math

import numpy as np
import jax
import jax.numpy as jnp
from jax import lax
from jax.experimental import pallas as pl
from jax.experimental.pallas import tpu as pltpu

D_MODEL = 1024
SSM_D_INNER = 2048
SSM_HEAD_DIM = 64
SSM_HEADS = 32
SSM_GROUPS = 4
SSM_HPG = 8
SSM_STATE = 128
SSM_CONV = 4
SSM_CHUNK = 128
SSM_CONV_DIM = 3072
NSA_HEADS = 16
NSA_HEAD_DIM = 64
NSA_WIDTH = 1024
NSA_KV_GROUPS = 2
NSA_HPG = 8
NSA_KV_WIDTH = 128
CMP_BLOCK = 32
CMP_STRIDE = 16
CMP_HIDDEN = 256
SEL_BLOCK = 64
SEL_TOPK = 4
WINDOW = 512
NSA_QBLOCK = 64
ROPE_THETA = 10000.0
MOE_GROUPS = 4
EXPERTS_PER_GROUP = 8
N_EXPERTS = 32
EXPERT_TOPK = 2
EXPERT_HIDDEN = 256
NORM_EPS = 1e-6

LANES = 128
NEG = -1e30
NSA_QTILE = 256
SLC_ROWS = 1024
WIN_KEYS = WINDOW + NSA_QTILE
WIN_PAD = WINDOW
KEY_CHUNK = 256
VMEM_LIMIT = 56 * 1024 * 1024

C_Z, C_X, C_BC, C_Q, C_GS, C_GA, C_KV, C_MAIN = 0, 2048, 4096, 5120, 6144, 7168, 8192, 8960

f32 = jnp.float32
bf16 = jnp.bfloat16


def _cparams(sem):
    return pltpu.CompilerParams(dimension_semantics=sem, vmem_limit_bytes=VMEM_LIMIT)


def _dot(a, b):
    return jnp.dot(a, b, preferred_element_type=f32)


def _dot_nt(a, b):
    return lax.dot_general(a, b, (((1,), (1,)), ((), ())), preferred_element_type=f32)


def _split3(x):
    hi = x.astype(bf16)
    r = x - hi.astype(f32)
    mid = r.astype(bf16)
    lo = (r - mid.astype(f32)).astype(bf16)
    return hi, mid, lo


def _dot_exact_rhs(a, b_bf16):
    hi, mid, lo = _split3(a)
    return _dot(hi, b_bf16) + _dot(mid, b_bf16) + _dot(lo, b_bf16)


def _dot_hp(a, b):
    ah = a.astype(bf16)
    al = (a - ah.astype(f32)).astype(bf16)
    bh = b.astype(bf16)
    bl = (b - bh.astype(f32)).astype(bf16)
    return _dot(ah, bh) + _dot(al, bh) + _dot(ah, bl)


def _silu(x):
    return x * jax.nn.sigmoid(x)


def _ada_kernel(c_ref, w_ref, b_ref, o_ref):
    o_ref[...] = _dot_hp(_silu(c_ref[...]), w_ref[...]) + b_ref[...]


def _ada(c, w_ada, b_ada):
    b, d = c.shape
    n = w_ada.shape[1]
    tn = 1024
    return pl.pallas_call(
        _ada_kernel,
        grid=(n // tn,),
        in_specs=[pl.BlockSpec((b, d), lambda j: (0, 0)),
                  pl.BlockSpec((d, tn), lambda j: (0, j)),
                  pl.BlockSpec((1, tn), lambda j: (0, j))],
        out_specs=pl.BlockSpec((b, tn), lambda j: (0, j)),
        out_shape=jax.ShapeDtypeStruct((b, n), f32),
        compiler_params=_cparams(("arbitrary",)),
    )(c, w_ada, b_ada.reshape(1, n))


def _rope_kernel(pos_ref, inv_ref, sgn_ref, cos_ref, sin_ref):
    ang = pos_ref[...].astype(f32) * inv_ref[...]
    cos_ref[...] = jnp.cos(ang)
    sin_ref[...] = jnp.sin(ang) * sgn_ref[...]


def _rope_tables(positions):
    t = positions.size
    tm = 2048
    inv = ROPE_THETA ** (-jnp.arange(0, NSA_HEAD_DIM, 2, dtype=f32) / NSA_HEAD_DIM)
    inv_full = jnp.tile(inv, LANES // 32).reshape(1, LANES)
    lane = np.arange(LANES)
    sgn = jnp.asarray(np.where((lane % 64) < 32, -1.0, 1.0), f32).reshape(1, LANES)
    return pl.pallas_call(
        _rope_kernel,
        grid=(t // tm,),
        in_specs=[pl.BlockSpec((tm, 1), lambda i: (i, 0)),
                  pl.BlockSpec((1, LANES), lambda i: (0, 0)),
                  pl.BlockSpec((1, LANES), lambda i: (0, 0))],
        out_specs=[pl.BlockSpec((tm, LANES), lambda i: (i, 0))] * 2,
        out_shape=[jax.ShapeDtypeStruct((t, LANES), f32)] * 2,
        compiler_params=_cparams(("arbitrary",)),
    )(positions.reshape(t, 1), inv_full, sgn)


def _hn_kernel(x_ref, ada_ref, g_ref, ws_ref, hn_ref, small_ref):
    x = x_ref[...]
    y = x * lax.rsqrt(jnp.mean(x * x, axis=-1, keepdims=True) + NORM_EPS) * g_ref[...]
    ada = ada_ref[0]
    hn = y * (1.0 + ada[1:2]) + ada[0:1]
    hn_ref[...] = hn.astype(bf16)
    small_ref[...] = _dot_hp(hn, ws_ref[...])


def _hn(x2, ada3, norm1_g, w_small, seq):
    t, d = x2.shape
    tm = 512
    return pl.pallas_call(
        _hn_kernel,
        grid=(t // tm,),
        in_specs=[pl.BlockSpec((tm, d), lambda i: (i, 0)),
                  pl.BlockSpec((1, 6, d), lambda i: ((i * tm) // seq, 0, 0)),
                  pl.BlockSpec((1, d), lambda i: (0, 0)),
                  pl.BlockSpec((d, LANES), lambda i: (0, 0))],
        out_specs=[pl.BlockSpec((tm, d), lambda i: (i, 0)),
                   pl.BlockSpec((tm, LANES), lambda i: (i, 0))],
        out_shape=[jax.ShapeDtypeStruct((t, d), bf16),
                   jax.ShapeDtypeStruct((t, LANES), f32)],
        compiler_params=_cparams(("arbitrary",)),
    )(x2, ada3, norm1_g.reshape(1, d), w_small)


def _mm_kernel(a_ref, b_ref, o_ref):
    o_ref[...] = _dot(a_ref[...], b_ref[...]).astype(o_ref.dtype)


def _inproj(hn, w_main):
    t, d = hn.shape
    n = w_main.shape[1]
    tm = 1024 if t % 1024 == 0 else t
    tn = 1792
    return pl.pallas_call(
        _mm_kernel,
        grid=(t // tm, n // tn),
        in_specs=[pl.BlockSpec((tm, d), lambda i, j: (i, 0)),
                  pl.BlockSpec((d, tn), lambda i, j: (0, j))],
        out_specs=pl.BlockSpec((tm, tn), lambda i, j: (i, j)),
        out_shape=jax.ShapeDtypeStruct((t, n), bf16),
        compiler_params=_cparams(("arbitrary", "arbitrary")),
    )(hn, w_main)


def _softplus(x):
    return jnp.maximum(x, 0.0) + jnp.log1p(jnp.exp(-jnp.abs(x)))


def _ssd_kernel(z_ref, xs_ref, bc_ref, sm_ref, cw_ref, cb_ref, dtb_ref, alog_ref, dsk_ref, ng_ref, e_ref, sh_ref,
                o_ref, xx, cbuf, state, ybuf):
    q = SSM_CHUNK
    n_st = SSM_STATE
    gw = SSM_HPG * SSM_HEAD_DIM

    @pl.when(pl.program_id(1) == 0)
    def _():
        xx[0:q, :] = jnp.zeros((q, SSM_CONV_DIM), bf16)
        state[...] = jnp.zeros(state.shape, f32)

    xx[q:2 * q, 0:SSM_D_INNER] = xs_ref[...]
    xx[q:2 * q, SSM_D_INNER:SSM_CONV_DIM] = bc_ref[...]
    cblk = 512
    for j in range(SSM_CONV_DIM // cblk):
        cols = slice(j * cblk, (j + 1) * cblk)
        shifted = _dot(sh_ref[...], xx[:, cols])
        acc = cb_ref[:, cols] + cw_ref[SSM_CONV - 1:SSM_CONV, cols] * xx[q:2 * q, cols].astype(f32)
        for k in range(SSM_CONV - 1):
            acc = acc + cw_ref[k:k + 1, cols] * shifted[k * q:(k + 1) * q]
        cbuf[:, cols] = _silu(acc)
    xx[0:q, :] = xx[q:2 * q, :]
    xc = cbuf[:, 0:SSM_D_INNER]
    bcm = cbuf[:, SSM_D_INNER:SSM_CONV_DIM]

    lane = lax.broadcasted_iota(jnp.int32, (1, LANES), 1)
    dt = _softplus(sm_ref[...] + dtb_ref[...])
    a = jnp.where(lane < SSM_HEADS, -jnp.exp(alog_ref[...]), 0.0)
    da = dt * a
    ri = lax.broadcasted_iota(jnp.int32, (q, q), 0)
    ci = lax.broadcasted_iota(jnp.int32, (q, q), 1)
    tril = ci <= ri
    tril_b = jnp.where(tril, 1.0, 0.0).astype(bf16)
    acum = _dot_exact_rhs_left(tril_b, da)
    acum_t = acum.T
    a_last = acum[q - 1:q, :]
    exp_a = jnp.exp(acum)
    w_end = jnp.exp(a_last - acum) * dt
    e_mat = e_ref[...]
    dt_x = _dot(dt.astype(bf16), e_mat)
    expa_x = _dot(exp_a.astype(bf16), e_mat)
    wend_x = _dot(w_end.astype(bf16), e_mat)
    elast_x = _dot_exact_rhs(jnp.broadcast_to(jnp.exp(a_last), (8, LANES)), e_mat)[0:1, :]
    xdt = (xc * dt_x).astype(bf16)
    xw = (xc * wend_x).astype(bf16)
    lo_mask = lane < SSM_HEAD_DIM

    for g in range(SSM_GROUPS):
        bg = bcm[:, g * n_st:(g + 1) * n_st]
        cg = bcm[:, SSM_GROUPS * n_st + g * n_st:SSM_GROUPS * n_st + (g + 1) * n_st]
        bg_b = bg.astype(bf16)
        cg_b = cg.astype(bf16)
        cb = _dot_nt(cg_b, bg_b)
        st = state[g]
        yoff = _dot(cg_b, st.astype(bf16)) * expa_x[:, g * gw:(g + 1) * gw]
        for hp in range(SSM_HPG // 2):
            pair = g * (SSM_HPG // 2) + hp
            xpair = xdt[:, pair * LANES:(pair + 1) * LANES]
            acc = yoff[:, hp * LANES:(hp + 1) * LANES]
            for par in range(2):
                h = 2 * pair + par
                xx = jnp.where(lo_mask, xpair, 0.0) if par == 0 else jnp.where(lo_mask, 0.0, xpair)
                seg = acum[:, h:h + 1] - acum_t[h:h + 1, :]
                decay = jnp.exp(jnp.where(tril, seg, NEG))
                acc = acc + _dot((cb * decay).astype(bf16), xx.astype(bf16))
            ybuf[:, pair * LANES:(pair + 1) * LANES] = acc
        state[g] = st * elast_x[:, g * gw:(g + 1) * gw] + _dot(bg.T.astype(bf16), xw[:, g * gw:(g + 1) * gw])

    z = z_ref[...].astype(f32)
    y = (ybuf[...] + dsk_ref[...] * xc) * _silu(z)
    for g in range(SSM_GROUPS):
        yg = y[:, g * gw:(g + 1) * gw]
        ms = jnp.mean(yg * yg, axis=-1, keepdims=True)
        o_ref[:, g * gw:(g + 1) * gw] = (yg * lax.rsqrt(ms + NORM_EPS) * ng_ref[:, g * gw:(g + 1) * gw]).astype(bf16)


def _dot_exact_rhs_left(l_bf16, x):
    hi, mid, lo = _split3(x)
    return _dot(l_bf16, hi) + _dot(l_bf16, mid) + _dot(l_bf16, lo)


def _ssd(main, small, conv_w, conv_b, dt_bias, a_log, d_skip, ssm_norm_g, batch, seq):
    t = main.shape[0]
    q = SSM_CHUNK
    nc = seq // q
    pad = LANES - SSM_HEADS
    dtb = jnp.pad(dt_bias, (0, pad)).reshape(1, LANES)
    alog = jnp.pad(a_log, (0, pad)).reshape(1, LANES)
    dsk = jnp.repeat(d_skip, SSM_HEAD_DIM).reshape(1, SSM_D_INNER)
    e_np = (np.arange(LANES)[:, None] == (np.arange(SSM_D_INNER)[None, :] // SSM_HEAD_DIM)).astype(np.float32)
    e_mat = jnp.asarray(e_np, bf16)
    tt = np.arange(q)
    sh_np = np.zeros(((SSM_CONV - 1) * q, 2 * q), np.float32)
    for k in range(SSM_CONV - 1):
        sh_np[k * q + tt, q + tt - (SSM_CONV - 1) + k] = 1.0
    sh_mat = jnp.asarray(sh_np, bf16)
    row = lambda b, c: (b * nc + c)
    const = lambda b, c: (0, 0)
    return pl.pallas_call(
        _ssd_kernel,
        grid=(batch, nc),
        in_specs=[pl.BlockSpec((q, SSM_D_INNER), lambda b, c: (row(b, c), C_Z // SSM_D_INNER)),
                  pl.BlockSpec((q, SSM_D_INNER), lambda b, c: (row(b, c), C_X // SSM_D_INNER)),
                  pl.BlockSpec((q, 1024), lambda b, c: (row(b, c), C_BC // 1024)),
                  pl.BlockSpec((q, LANES), lambda b, c: (row(b, c), 0)),
                  pl.BlockSpec((SSM_CONV, SSM_CONV_DIM), const),
                  pl.BlockSpec((1, SSM_CONV_DIM), const),
                  pl.BlockSpec((1, LANES), const),
                  pl.BlockSpec((1, LANES), const),
                  pl.BlockSpec((1, SSM_D_INNER), const),
                  pl.BlockSpec((1, SSM_D_INNER), const),
                  pl.BlockSpec((LANES, SSM_D_INNER), const),
                  pl.BlockSpec(((SSM_CONV - 1) * q, 2 * q), const)],
        out_specs=pl.BlockSpec((q, SSM_D_INNER), lambda b, c: (row(b, c), 0)),
        out_shape=jax.ShapeDtypeStruct((t, SSM_D_INNER), bf16),
        scratch_shapes=[pltpu.VMEM((2 * q, SSM_CONV_DIM), bf16),
                        pltpu.VMEM((q, SSM_CONV_DIM), f32),
                        pltpu.VMEM((SSM_GROUPS, SSM_STATE, SSM_HPG * SSM_HEAD_DIM), f32),
                        pltpu.VMEM((q, SSM_D_INNER), f32)],
        compiler_params=_cparams(("arbitrary", "arbitrary")),
    )(main, main, main, small, conv_w, conv_b.reshape(1, -1), dtb, alog, dsk,
      ssm_norm_g.reshape(1, -1), e_mat, sh_mat)


def _nsa_kernel(q_ref, kc_ref, vc_ref, ks_ref, vs_ref, kw_ref, vw_ref, sm_ref, cos_ref, sin_ref,
                pek_ref, w1k_ref, w2k_ref, pev_ref, w1v_ref, w2v_ref, ovl_ref, eg_ref,
                o_ref,
                ks_p, vs_p, kw_p, vw_p, kc_buf, vc_buf, kcmp_p, vcmp_p, q_buf, c_buf, b_buf, s_buf, mx_buf, o_acc):
    g = pl.program_id(1)
    seq = q_ref.shape[0]
    qb = NSA_QTILE
    dk = NSA_HEAD_DIM
    n_cmp = seq // CMP_STRIDE
    n_pairs = NSA_HPG // 2
    rows = NSA_HPG * qb
    slc_rows = s_buf.shape[1]
    n_key_chunks = seq // KEY_CHUNK
    sel_shift = SEL_BLOCK.bit_length() - 1
    scale = dk ** -0.5

    lane = lax.broadcasted_iota(jnp.int32, (1, LANES), 1)
    lo = lane < dk
    first_half = (lane & (dk - 1)) < dk // 2

    def rope(tv, cs, sn):
        rot = jnp.where(first_half, pltpu.roll(tv, LANES - dk // 2, 1), pltpu.roll(tv, dk // 2, 1))
        return tv * cs + rot * sn

    def group_lanes(tv):
        return jnp.where(lo, jnp.where(g == 0, tv, pltpu.roll(tv, dk, 1)), 0.0)

    def with_ones(tv):
        return jnp.where(lane == dk, 1.0, tv)

    def normalise(acc):
        return jnp.where(lo, acc / acc[:, dk:dk + 1], 0.0)

    prep_rows = 256

    def prep(i, carry):
        r0 = pl.multiple_of(i * prep_rows, prep_rows)
        sl = pl.ds(r0, prep_rows)
        cs = cos_ref[sl, :]
        sn = sin_ref[sl, :]
        ks_p[sl, :] = group_lanes(rope(ks_ref[sl, :].astype(f32), cs, sn)).astype(bf16)
        vs_p[sl, :] = with_ones(group_lanes(vs_ref[sl, :].astype(f32))).astype(bf16)
        slw = pl.ds(pl.multiple_of(r0 + WIN_PAD, 64), prep_rows)
        kw_p[slw, :] = group_lanes(rope(kw_ref[sl, :].astype(f32), cs, sn)).astype(bf16)
        vw_p[slw, :] = group_lanes(vw_ref[sl, :].astype(f32)).astype(bf16)
        kc_buf[sl, :] = group_lanes(rope(kc_ref[sl, :].astype(f32), cs, sn))
        vc_buf[sl, :] = group_lanes(vc_ref[sl, :].astype(f32))
        return carry

    lax.fori_loop(0, seq // prep_rows, prep, 0)
    kw_p[0:WIN_PAD, :] = jnp.zeros((WIN_PAD, LANES), bf16)
    vw_p[0:WIN_PAD, :] = jnp.zeros((WIN_PAD, LANES), bf16)
    kc_buf[seq:seq + CMP_BLOCK, :] = jnp.zeros((CMP_BLOCK, LANES), f32)
    vc_buf[seq:seq + CMP_BLOCK, :] = jnp.zeros((CMP_BLOCK, LANES), f32)

    def compress(buf, pe_ref, w1_ref, w2_ref, out):
        pre = jnp.zeros((n_cmp, CMP_HIDDEN), f32)
        for l in range(CMP_BLOCK):
            xl = buf[pl.ds(l, n_cmp, stride=CMP_STRIDE), :] + pe_ref[l:l + 1, :]
            pre = pre + _dot(xl.astype(bf16), w1_ref[l * LANES:(l + 1) * LANES, :])
        out[...] = _dot(_silu(pre).astype(bf16), w2_ref[...]).astype(bf16)

    compress(kc_buf, pek_ref, w1k_ref, w2k_ref, kcmp_p)
    compress(vc_buf, pev_ref, w1v_ref, w2v_ref, vcmp_p)

    def tile8(x):
        return jnp.concatenate([x] * NSA_HPG, axis=0)

    wr = lax.broadcasted_iota(jnp.int32, (qb, WIN_KEYS), 0)
    wc = lax.broadcasted_iota(jnp.int32, (qb, WIN_KEYS), 1)

    def stage_a(ti, slot):
        s0 = pl.multiple_of(ti * qb, qb)
        sl = pl.ds(s0, qb)
        tq = s0 + lax.broadcasted_iota(jnp.int32, (qb, 1), 0)

        q2 = q_ref[sl, :].astype(f32)
        qs = jnp.concatenate([q2[:, p * LANES:(p + 1) * LANES] for p in range(n_pairs)], axis=0)
        cs = jnp.concatenate([cos_ref[sl, :]] * n_pairs, axis=0)
        sn = jnp.concatenate([sin_ref[sl, :]] * n_pairs, axis=0)
        qr = rope(qs, cs, sn) * scale
        q_even = jnp.where(lo, qr, 0.0)
        q_odd = jnp.where(lo, pltpu.roll(qr, dk, 1), 0.0)
        qall = jnp.concatenate([q_even, q_odd], axis=0).astype(bf16)

        sc = _dot_nt(qall, kcmp_p[...])
        ccol = lax.broadcasted_iota(jnp.int32, (qb, n_cmp), 1)
        cmask = (ccol * CMP_STRIDE + (CMP_BLOCK - 1)) <= tq
        cbias = tile8(jnp.where(cmask, 0.0, NEG))
        cmf = tile8(jnp.where(cmask, 1.0, 0.0))
        scb = sc + cbias
        pc = jnp.exp(scb - jnp.max(scb, axis=-1, keepdims=True)) * cmf
        pc = pc / jnp.maximum(jnp.sum(pc, axis=-1, keepdims=True), 1e-30)
        o_cmp = _dot(pc.astype(bf16), vcmp_p[...])
        pcs = pc[0:qb]
        for r in range(1, NSA_HPG):
            pcs = pcs + pc[r * qb:(r + 1) * qb]
        imp = _dot_exact_rhs(pcs, ovl_ref[...])

        jl = lax.broadcasted_iota(jnp.int32, (qb, LANES), 1)
        local = tq >> sel_shift
        forced = (jl == local) | (jl == 0)
        val = jnp.where(forced | (jl > local), -jnp.inf, imp)
        sel = jnp.where(forced, 1.0, 0.0)
        for _ in range(SEL_TOPK - 2):
            hit = (jl == jnp.argmax(val, axis=-1, keepdims=True)) & (val > -jnp.inf)
            sel = jnp.where(hit, 1.0, sel)
            val = jnp.where(hit, -jnp.inf, val)
        sel_b = sel.astype(bf16)

        q_buf[slot] = qall
        c_buf[slot] = o_cmp
        er = lax.broadcasted_iota(jnp.int32, (LANES, KEY_CHUNK), 0)
        ec = lax.broadcasted_iota(jnp.int32, (LANES, KEY_CHUNK), 1)
        kr = lax.broadcasted_iota(jnp.int32, (qb, KEY_CHUNK), 1)
        for c in range(n_key_chunks):
            expand = jnp.where(er == c * (KEY_CHUNK // SEL_BLOCK) + (ec >> sel_shift), 1.0, 0.0).astype(bf16)
            ok_s = (_dot(sel_b, expand) > 0.5) & ((c * KEY_CHUNK + kr) <= tq)
            b_buf[slot, c] = jnp.where(ok_s, 0.0, NEG)

    half = n_pairs * qb

    def qblock(qi, carry):
        slot = 0
        stage_a(qi, slot)
        s0 = pl.multiple_of(qi * qb, qb)
        sl = pl.ds(s0, qb)
        qall = q_buf[slot]

        kk = kw_p[pl.ds(s0, WIN_KEYS), :]
        vv = vw_p[pl.ds(s0, WIN_KEYS), :]
        rel = wc - WIN_PAD - wr
        ok = (rel <= 0) & (rel > -WINDOW) & ((wc - WIN_PAD + s0) >= 0)
        wbias = jnp.concatenate([jnp.where(ok, 0.0, NEG)] * 2, axis=0)
        for hr in range(rows // (2 * qb)):
            rs = slice(hr * 2 * qb, (hr + 1) * 2 * qb)
            sw = _dot_nt(qall[rs], kk) + wbias
            pw = jnp.exp(sw - jnp.max(sw, axis=-1, keepdims=True))
            o_acc[1, rs, :] = _dot(pw.astype(bf16), vv) / jnp.sum(pw, axis=-1, keepdims=True)

        n_chunks = ((s0 + qb - 1) >> (KEY_CHUNK.bit_length() - 1)) + 1
        n_steps = (n_chunks + 1) >> 1
        mx_buf[...] = jnp.full((rows, LANES), NEG, f32)
        o_acc[0] = jnp.zeros((rows, LANES), f32)
        for hv in range(rows // slc_rows):
            rs = slice(hv * slc_rows, (hv + 1) * slc_rows)

            def pass_a(step, carry_a, rs=rs):
                mx = mx_buf[rs, :]
                for u in range(2):
                    c = 2 * step + u
                    k0 = pl.multiple_of(c * KEY_CHUNK, KEY_CHUNK)
                    bias = jnp.concatenate([b_buf[slot, c]] * (slc_rows // qb), axis=0)
                    s = _dot_nt(q_buf[slot, rs, :], ks_p[pl.ds(k0, KEY_CHUNK), :]) + bias
                    s_buf[c] = s
                    mx = jnp.maximum(mx, jnp.maximum(s[:, :LANES], s[:, LANES:]))
                mx_buf[rs, :] = mx
                return carry_a

            lax.fori_loop(0, n_steps, pass_a, 0)
            mx_buf[rs, :] = jnp.broadcast_to(jnp.max(mx_buf[rs, :], axis=-1, keepdims=True), (slc_rows, LANES))

            def pass_b(step, carry_b, rs=rs):
                mb = jnp.concatenate([mx_buf[rs, :]] * (KEY_CHUNK // LANES), axis=1)
                acc = o_acc[0, rs, :]
                for u in range(2):
                    c = 2 * step + u
                    k0 = pl.multiple_of(c * KEY_CHUNK, KEY_CHUNK)
                    p = jnp.exp((s_buf[c] - mb).astype(bf16))
                    acc = acc + _dot(p, vs_p[pl.ds(k0, KEY_CHUNK), :])
                o_acc[0, rs, :] = acc
                return carry_b

            lax.fori_loop(0, n_steps, pass_b, 0)
        o_cmp = c_buf[slot]
        o_slc = normalise(o_acc[0])
        o_win = o_acc[1]

        gates = _dot(jax.nn.sigmoid(sm_ref[sl, :]).astype(bf16), eg_ref[0])

        def pair_tile(o, p):
            return o[p * qb:(p + 1) * qb] + pltpu.roll(o[half + p * qb:half + (p + 1) * qb], dk, 1)

        width = NSA_HPG * dk
        for p in range(n_pairs):
            cols = slice(p * LANES, (p + 1) * LANES)
            out = (gates[:, 0 * width + p * LANES:0 * width + (p + 1) * LANES] * pair_tile(o_cmp, p)
                   + gates[:, 1 * width + p * LANES:1 * width + (p + 1) * LANES] * pair_tile(o_slc, p)
                   + gates[:, 2 * width + p * LANES:2 * width + (p + 1) * LANES] * pair_tile(o_win, p))
            o_ref[sl, cols] = out.astype(bf16)
        return carry

    lax.fori_loop(0, seq // qb, qblock, 0)


def _pad_w1(w1):
    h = w1.shape[1]
    w = w1.reshape(CMP_BLOCK, NSA_HEAD_DIM, h)
    return jnp.pad(w, ((0, 0), (0, LANES - NSA_HEAD_DIM), (0, 0))).reshape(CMP_BLOCK * LANES, h).astype(bf16)


def _nsa(main, small, cos_t, sin_t, cmp_pe_k, cmp_w1_k, cmp_w2_k, cmp_pe_v, cmp_w1_v, cmp_w2_v, batch, seq):
    t = main.shape[0]
    assert seq % (2 * KEY_CHUNK) == 0 and seq // SEL_BLOCK <= LANES
    dk = NSA_HEAD_DIM
    width = NSA_HPG * dk
    n_cmp = seq // CMP_STRIDE
    pad_l = ((0, 0), (0, LANES - dk))
    pek = jnp.pad(cmp_pe_k, pad_l)
    pev = jnp.pad(cmp_pe_v, pad_l)
    w1k = _pad_w1(cmp_w1_k)
    w1v = _pad_w1(cmp_w1_v)
    w2k = jnp.pad(cmp_w2_k, pad_l).astype(bf16)
    w2v = jnp.pad(cmp_w2_v, pad_l).astype(bf16)
    cstart = np.arange(LANES)[:, None] * CMP_STRIDE
    sstart = np.arange(LANES)[None, :] * SEL_BLOCK
    valid_c = np.arange(LANES)[:, None] < (seq - CMP_BLOCK) // CMP_STRIDE + 1
    valid_j = np.arange(LANES)[None, :] < seq // SEL_BLOCK
    ovl = ((cstart < sstart + SEL_BLOCK) & (cstart + CMP_BLOCK > sstart) & valid_c & valid_j).astype(np.float32)
    ovl = jnp.asarray(ovl[:n_cmp] if n_cmp <= LANES else ovl, bf16)
    eg = np.zeros((NSA_KV_GROUPS, LANES, 3 * width), np.float32)
    for gg in range(NSA_KV_GROUPS):
        for r in range(NSA_HPG):
            for j in range(3):
                eg[gg, SSM_HEADS + gg * NSA_HPG * 3 + r * 3 + j, j * width + r * dk:j * width + (r + 1) * dk] = 1.0
    eg = jnp.asarray(eg, bf16)

    kv_blk = C_KV // LANES
    kv_spec = lambda j: pl.BlockSpec((seq, LANES), lambda b, g: (b, kv_blk + j))
    const2 = lambda b, g: (0, 0)
    return pl.pallas_call(
        _nsa_kernel,
        grid=(batch, NSA_KV_GROUPS),
        in_specs=[pl.BlockSpec((seq, width), lambda b, g: (b, C_Q // width + g))]
                 + [kv_spec(j) for j in range(6)]
                 + [pl.BlockSpec((seq, LANES), lambda b, g: (b, 0))] * 3
                 + [pl.BlockSpec((CMP_BLOCK, LANES), const2),
                    pl.BlockSpec((CMP_BLOCK * LANES, CMP_HIDDEN), const2),
                    pl.BlockSpec((CMP_HIDDEN, LANES), const2)] * 2
                 + [pl.BlockSpec(ovl.shape, const2),
                    pl.BlockSpec((1, LANES, 3 * width), lambda b, g: (g, 0, 0))],
        out_specs=pl.BlockSpec((seq, width), lambda b, g: (b, g)),
        out_shape=jax.ShapeDtypeStruct((t, NSA_WIDTH), bf16),
        scratch_shapes=[pltpu.VMEM((seq, LANES), bf16),
                        pltpu.VMEM((seq, LANES), bf16),
                        pltpu.VMEM((seq + WIN_PAD, LANES), bf16),
                        pltpu.VMEM((seq + WIN_PAD, LANES), bf16),
                        pltpu.VMEM((seq + CMP_BLOCK, LANES), f32),
                        pltpu.VMEM((seq + CMP_BLOCK, LANES), f32),
                        pltpu.VMEM((n_cmp, LANES), bf16),
                        pltpu.VMEM((n_cmp, LANES), bf16),
                        pltpu.VMEM((1, NSA_HPG * NSA_QTILE, LANES), bf16),
                        pltpu.VMEM((1, NSA_HPG * NSA_QTILE, LANES), f32),
                        pltpu.VMEM((1, seq // KEY_CHUNK, NSA_QTILE, KEY_CHUNK), f32),
                        pltpu.VMEM((seq // KEY_CHUNK, min(NSA_HPG * NSA_QTILE, SLC_ROWS), KEY_CHUNK), f32),
                        pltpu.VMEM((NSA_HPG * NSA_QTILE, LANES), f32),
                        pltpu.VMEM((2, NSA_HPG * NSA_QTILE, LANES), f32)],
        compiler_params=_cparams(("arbitrary", "arbitrary")),
    )(main, main, main, main, main, main, main, small, cos_t, sin_t,
      pek, w1k, w2k, pev, w1v, w2v, ovl, eg)


def _post_kernel(yn_ref, on_ref, gs_ref, ga_ref, x_ref, ada_ref, wso_ref, wno_ref, wo_ref, g2_ref, wr_ref, br_ref,
                 h1_ref, hn2_ref, cw_ref):
    tm = x_ref.shape[0]
    y_ssm = _dot(yn_ref[...], wso_ref[...])
    y_nsa = _dot(on_ref[...], wno_ref[...])
    merged = jax.nn.sigmoid(gs_ref[...].astype(f32)) * y_ssm + jax.nn.sigmoid(ga_ref[...].astype(f32)) * y_nsa
    ada = ada_ref[0]
    h1 = x_ref[...] + ada[2:3] * _dot(merged.astype(bf16), wo_ref[...])
    h1_ref[...] = h1
    hn2 = h1 * lax.rsqrt(jnp.mean(h1 * h1, axis=-1, keepdims=True) + NORM_EPS) * g2_ref[...]
    hn2 = hn2 * (1.0 + ada[4:5]) + ada[3:4]
    hn2_ref[...] = hn2.astype(bf16)

    logits = _dot_hp(hn2, wr_ref[...]) + br_ref[...]
    lane = lax.broadcasted_iota(jnp.int32, (tm, LANES), 1)
    is_g = lane < MOE_GROUPS
    lg = jnp.where(is_g, logits, NEG)
    eg = jnp.where(is_g, jnp.exp(lg - jnp.max(lg, axis=-1, keepdims=True)), 0.0)
    pg_all = eg / jnp.sum(eg, axis=-1, keepdims=True)
    pg = jnp.max(pg_all, axis=-1, keepdims=True)
    gsel = jnp.min(jnp.where(is_g & (pg_all == pg), lane, LANES), axis=-1, keepdims=True)
    member = (lane >= MOE_GROUPS) & (lane < MOE_GROUPS + N_EXPERTS) & (((lane - MOE_GROUPS) >> 3) == gsel)
    le = jnp.where(member, logits, NEG)
    ee = jnp.where(member, jnp.exp(le - jnp.max(le, axis=-1, keepdims=True)), 0.0)
    pe = jnp.where(member, ee / jnp.sum(ee, axis=-1, keepdims=True), -1.0)
    v0 = jnp.max(pe, axis=-1, keepdims=True)
    i0 = jnp.min(jnp.where(pe == v0, lane, LANES), axis=-1, keepdims=True)
    hit0 = lane == i0
    pe1 = jnp.where(hit0, -1.0, pe)
    v1 = jnp.max(pe1, axis=-1, keepdims=True)
    i1 = jnp.min(jnp.where(pe1 == v1, lane, LANES), axis=-1, keepdims=True)
    hit1 = lane == i1
    den = v0 + v1
    w0 = pg * v0 / den
    w1 = pg * v1 / den

    cw_ref[...] = jnp.where(hit0, w0, jnp.where(hit1, w1, 0.0))


def _post(yn, o_nsa, main, x2, ada3, w_ssm_out, w_nsa_out, w_o, norm2_g, w_route, b_route, seq):
    t, d = x2.shape
    tm = 512
    const = lambda i: (0, 0)
    return pl.pallas_call(
        _post_kernel,
        grid=(t // tm,),
        in_specs=[pl.BlockSpec((tm, SSM_D_INNER), lambda i: (i, 0)),
                  pl.BlockSpec((tm, NSA_WIDTH), lambda i: (i, 0)),
                  pl.BlockSpec((tm, d), lambda i: (i, C_GS // d)),
                  pl.BlockSpec((tm, d), lambda i: (i, C_GA // d)),
                  pl.BlockSpec((tm, d), lambda i: (i, 0)),
                  pl.BlockSpec((1, 6, d), lambda i: ((i * tm) // seq, 0, 0)),
                  pl.BlockSpec((SSM_D_INNER, d), const),
                  pl.BlockSpec((NSA_WIDTH, d), const),
                  pl.BlockSpec((d, d), const),
                  pl.BlockSpec((1, d), const),
                  pl.BlockSpec((d, LANES), const),
                  pl.BlockSpec((1, LANES), const)],
        out_specs=[pl.BlockSpec((tm, d), lambda i: (i, 0)),
                   pl.BlockSpec((tm, d), lambda i: (i, 0)),
                   pl.BlockSpec((tm, LANES), lambda i: (i, 0))],
        out_shape=[jax.ShapeDtypeStruct((t, d), f32),
                   jax.ShapeDtypeStruct((t, d), bf16),
                   jax.ShapeDtypeStruct((t, LANES), f32)],
        compiler_params=_cparams(("arbitrary",)),
    )(yn, o_nsa, main, main, x2, ada3, w_ssm_out, w_nsa_out, w_o, norm2_g.reshape(1, d), w_route, b_route)


def _moe_kernel(h1_ref, x_ref, cw_ref, ada_ref, wgu_ref, wd_ref, fg_ref, o_ref, y_acc, hcat):
    g = pl.program_id(1)
    tm = x_ref.shape[0]
    hd = EXPERT_HIDDEN

    @pl.when(g == 0)
    def _():
        y_acc[...] = jnp.zeros(y_acc.shape, f32)

    x = x_ref[...]
    cw = cw_ref[...]
    lane = lax.broadcasted_iota(jnp.int32, (tm, LANES), 1)
    for e in range(EXPERTS_PER_GROUP):
        hgu = _dot(x, wgu_ref[0, :, 2 * e * hd:2 * (e + 1) * hd])
        ce = jnp.sum(jnp.where(lane == MOE_GROUPS + g * EXPERTS_PER_GROUP + e, cw, 0.0), axis=-1, keepdims=True)
        hcat[:, e * hd:(e + 1) * hd] = (_silu(hgu[:, :hd]) * hgu[:, hd:] * ce).astype(bf16)
    y_acc[...] += _dot(hcat[...], wd_ref[0])

    @pl.when(g == MOE_GROUPS - 1)
    def _():
        h2 = h1_ref[...] + ada_ref[0][5:6] * y_acc[...]
        o_ref[...] = h2 * lax.rsqrt(jnp.mean(h2 * h2, axis=-1, keepdims=True) + NORM_EPS) * fg_ref[...]


def _moe(h1, hn2, cw, ada3, w_gate, w_up, w_down, final_g, seq):
    t, d = h1.shape
    tm = 512
    hd = EXPERT_HIDDEN
    epg = EXPERTS_PER_GROUP
    wgu = jnp.concatenate([w_gate, w_up], axis=-1).astype(bf16).reshape(MOE_GROUPS, epg, d, 2 * hd)
    wgu = wgu.transpose(0, 2, 1, 3).reshape(MOE_GROUPS, d, epg * 2 * hd)
    wd = w_down.astype(bf16).reshape(MOE_GROUPS, epg * hd, d)
    return pl.pallas_call(
        _moe_kernel,
        grid=(t // tm, MOE_GROUPS),
        in_specs=[pl.BlockSpec((tm, d), lambda i, g: (i, 0)),
                  pl.BlockSpec((tm, d), lambda i, g: (i, 0)),
                  pl.BlockSpec((tm, LANES), lambda i, g: (i, 0)),
                  pl.BlockSpec((1, 6, d), lambda i, g: ((i * tm) // seq, 0, 0)),
                  pl.BlockSpec((1, d, epg * 2 * hd), lambda i, g: (g, 0, 0)),
                  pl.BlockSpec((1, epg * hd, d), lambda i, g: (g, 0, 0)),
                  pl.BlockSpec((1, d), lambda i, g: (0, 0))],
        out_specs=pl.BlockSpec((tm, d), lambda i, g: (i, 0)),
        out_shape=jax.ShapeDtypeStruct((t, d), f32),
        scratch_shapes=[pltpu.VMEM((tm, d), f32),
                        pltpu.VMEM((tm, epg * hd), bf16)],
        compiler_params=_cparams(("arbitrary", "arbitrary")),
    )(h1, hn2, cw, ada3, wgu, wd, final_g.reshape(1, d))


def _pack_w_in(w_in):
    o = np.cumsum([0, SSM_D_INNER, SSM_CONV_DIM, SSM_HEADS, NSA_WIDTH] + [NSA_KV_WIDTH] * 6
                  + [3 * NSA_HEADS, D_MODEL, D_MODEL])
    z_xbc = w_in[:, o[0]:o[2]]
    w_dt = w_in[:, o[2]:o[3]]
    w_q = w_in[:, o[3]:o[4]]
    w_kv = w_in[:, o[4]:o[10]]
    w_gn = w_in[:, o[10]:o[11]]
    w_gsa = w_in[:, o[11]:o[13]]
    w_main = jnp.concatenate([z_xbc, w_q, w_gsa, w_kv], axis=1).astype(bf16)
    w_small = jnp.concatenate(
        [w_dt, w_gn, jnp.zeros((w_in.shape[0], LANES - SSM_HEADS - 3 * NSA_HEADS), f32)], axis=1)
    return w_main, w_small


def _layer(h, ada3, cos_t, sin_t, batch, seq, norm1_g, w_in, conv_w, conv_b, dt_bias, a_log, d_skip, ssm_norm_g,
           w_ssm_out, cmp_pe_k, cmp_w1_k, cmp_w2_k, cmp_pe_v, cmp_w1_v, cmp_w2_v, w_nsa_out, w_o, norm2_g,
           w_router_group, b_router_group, w_router_expert, b_router_expert, w_exp_gate, w_exp_up, w_exp_down,
           final_g):
    t, d = h.shape
    w_main, w_small = _pack_w_in(w_in)
    hn, small = _hn(h, ada3, norm1_g, w_small, seq)
    main = _inproj(hn, w_main)
    yn = _ssd(main, small, conv_w, conv_b, dt_bias, a_log, d_skip, ssm_norm_g, batch, seq)
    o_nsa = _nsa(main, small, cos_t, sin_t, cmp_pe_k, cmp_w1_k, cmp_w2_k, cmp_pe_v, cmp_w1_v, cmp_w2_v, batch, seq)
    rpad = LANES - MOE_GROUPS - N_EXPERTS
    w_route = jnp.concatenate([w_router_group, w_router_expert, jnp.zeros((d, rpad), f32)], axis=1)
    b_route = jnp.concatenate([b_router_group, b_router_expert, jnp.zeros((rpad,), f32)]).reshape(1, LANES)
    h1, hn2, cw = _post(yn, o_nsa, main, h, ada3, w_ssm_out.astype(bf16), w_nsa_out.astype(bf16),
                        w_o.astype(bf16), norm2_g, w_route, b_route, seq)
    return _moe(h1, hn2, cw, ada3, w_exp_gate, w_exp_up, w_exp_down, final_g, seq)


def kernel(x, c, positions, w_ada, b_ada, norm1_g, w_in, conv_w, conv_b, dt_bias, a_log, d_skip, ssm_norm_g, w_ssm_out, cmp_pe_k, cmp_w1_k, cmp_w2_k, cmp_pe_v, cmp_w1_v, cmp_w2_v, w_nsa_out, w_o, norm2_g, w_router_group, b_router_group, w_router_expert, b_router_expert, w_exp_gate, w_exp_up, w_exp_down, final_g):
    batch, seq, d = x.shape
    depth = w_ada.shape[0]
    assert depth == 1, "final RMSNorm is fused into the single layer's MoE combine"
    cos_t, sin_t = _rope_tables(positions)
    h = x.reshape(batch * seq, d)
    l = 0
    ada3 = _ada(c, w_ada[l], b_ada[l]).reshape(batch, 6, d)
    out = _layer(h, ada3, cos_t, sin_t, batch, seq, norm1_g[l], w_in[l], conv_w[l], conv_b[l], dt_bias[l],
                 a_log[l], d_skip[l], ssm_norm_g[l], w_ssm_out[l], cmp_pe_k[l], cmp_w1_k[l], cmp_w2_k[l],
                 cmp_pe_v[l], cmp_w1_v[l], cmp_w2_v[l], w_nsa_out[l], w_o[l], norm2_g[l], w_router_group[l],
                 b_router_group[l], w_router_expert[l], b_router_expert[l], w_exp_gate[l], w_exp_up[l],
                 w_exp_down[l], final_g)
    return out.reshape(batch, seq, d)
```

```python
import functools
import math

import numpy as np
import jax
import jax.numpy as jnp
from jax import lax
from jax.experimental import pallas as pl
from jax.experimental.pallas import tpu as pltpu

D_MODEL = 1024
SSM_D_INNER = 2048
SSM_HEAD_DIM = 64
SSM_HEADS = 32
SSM_GROUPS = 4
SSM_HPG = 8
SSM_STATE = 128
SSM_CONV = 4
SSM_CHUNK = 128
SSM_CONV_DIM = 3072
NSA_HEADS = 16
NSA_HEAD_DIM = 64
NSA_WIDTH = 1024
NSA_KV_GROUPS = 2
NSA_HPG = 8
NSA_KV_WIDTH = 128
CMP_BLOCK = 32
CMP_STRIDE = 16
CMP_HIDDEN = 256
SEL_BLOCK = 64
SEL_TOPK = 4
WINDOW = 512
NSA_QBLOCK = 64
ROPE_THETA = 10000.0
MOE_GROUPS = 4
EXPERTS_PER_GROUP = 8
N_EXPERTS = 32
EXPERT_TOPK = 2
EXPERT_HIDDEN = 256
NORM_EPS = 1e-6

LANES = 128
NEG = -1e30
NSA_QTILE = 256
SLC_ROWS = 1024
WIN_KEYS = WINDOW + NSA_QTILE
WIN_PAD = WINDOW
KEY_CHUNK = 256
VMEM_LIMIT = 56 * 1024 * 1024

C_Z, C_X, C_BC, C_Q, C_GS, C_GA, C_KV, C_MAIN = 0, 2048, 4096, 5120, 6144, 7168, 8192, 8960

f32 = jnp.float32
bf16 = jnp.bfloat16


def _cparams(sem):
    return pltpu.CompilerParams(dimension_semantics=sem, vmem_limit_bytes=VMEM_LIMIT)


def _dot(a, b):
    return jnp.dot(a, b, preferred_element_type=f32)


def _dot_nt(a, b):
    return lax.dot_general(a, b, (((1,), (1,)), ((), ())), preferred_element_type=f32)


def _split3(x):
    hi = x.astype(bf16)
    r = x - hi.astype(f32)
    mid = r.astype(bf16)
    lo = (r - mid.astype(f32)).astype(bf16)
    return hi, mid, lo


def _dot_exact_rhs(a, b_bf16):
    hi, mid, lo = _split3(a)
    return _dot(hi, b_bf16) + _dot(mid, b_bf16) + _dot(lo, b_bf16)


def _dot_hp(a, b):
    ah = a.astype(bf16)
    al = (a - ah.astype(f32)).astype(bf16)
    bh = b.astype(bf16)
    bl = (b - bh.astype(f32)).astype(bf16)
    return _dot(ah, bh) + _dot(al, bh) + _dot(ah, bl)


def _silu(x):
    return x * jax.nn.sigmoid(x)


def _ada_kernel(c_ref, w_ref, b_ref, o_ref):
    o_ref[...] = _dot_hp(_silu(c_ref[...]), w_ref[...]) + b_ref[...]


def _ada(c, w_ada, b_ada):
    b, d = c.shape
    n = w_ada.shape[1]
    tn = 1024
    return pl.pallas_call(
        _ada_kernel,
        grid=(n // tn,),
        in_specs=[pl.BlockSpec((b, d), lambda j: (0, 0)),
                  pl.BlockSpec((d, tn), lambda j: (0, j)),
                  pl.BlockSpec((1, tn), lambda j: (0, j))],
        out_specs=pl.BlockSpec((b, tn), lambda j: (0, j)),
        out_shape=jax.ShapeDtypeStruct((b, n), f32),
        compiler_params=_cparams(("arbitrary",)),
    )(c, w_ada, b_ada.reshape(1, n))


def _rope_kernel(pos_ref, inv_ref, sgn_ref, cos_ref, sin_ref):
    ang = pos_ref[...].astype(f32) * inv_ref[...]
    cos_ref[...] = jnp.cos(ang)
    sin_ref[...] = jnp.sin(ang) * sgn_ref[...]


def _rope_tables(positions):
    t = positions.size
    tm = 2048
    inv = ROPE_THETA ** (-jnp.arange(0, NSA_HEAD_DIM, 2, dtype=f32) / NSA_HEAD_DIM)
    inv_full = jnp.tile(inv, LANES // 32).reshape(1, LANES)
    lane = np.arange(LANES)
    sgn = jnp.asarray(np.where((lane % 64) < 32, -1.0, 1.0), f32).reshape(1, LANES)
    return pl.pallas_call(
        _rope_kernel,
        grid=(t // tm,),
        in_specs=[pl.BlockSpec((tm, 1), lambda i: (i, 0)),
                  pl.BlockSpec((1, LANES), lambda i: (0, 0)),
                  pl.BlockSpec((1, LANES), lambda i: (0, 0))],
        out_specs=[pl.BlockSpec((tm, LANES), lambda i: (i, 0))] * 2,
        out_shape=[jax.ShapeDtypeStruct((t, LANES), f32)] * 2,
        compiler_params=_cparams(("arbitrary",)),
    )(positions.reshape(t, 1), inv_full, sgn)


def _inproj_kernel(x_ref, ada_ref, g_ref, ws_ref, w_ref, main_ref, small_ref, hn_buf):
    @pl.when(pl.program_id(1) == 0)
    def _():
        x = x_ref[...]
        y = x * lax.rsqrt(jnp.mean(x * x, axis=-1, keepdims=True) + NORM_EPS) * g_ref[...]
        ada = ada_ref[0]
        hn = y * (1.0 + ada[1:2]) + ada[0:1]
        hn_buf[...] = hn.astype(bf16)
        small_ref[...] = _dot_hp(hn, ws_ref[...])

    main_ref[...] = _dot(hn_buf[...], w_ref[...]).astype(main_ref.dtype)


def _inproj(x2, ada3, norm1_g, w_small, w_main, seq):
    t, d = x2.shape
    n = w_main.shape[1]
    tm = 1024
    tn = 1792
    return pl.pallas_call(
        _inproj_kernel,
        grid=(t // tm, n // tn),
        in_specs=[pl.BlockSpec((tm, d), lambda i, j: (i, 0)),
                  pl.BlockSpec((1, 6, d), lambda i, j: ((i * tm) // seq, 0, 0)),
                  pl.BlockSpec((1, d), lambda i, j: (0, 0)),
                  pl.BlockSpec((d, LANES), lambda i, j: (0, 0)),
                  pl.BlockSpec((d, tn), lambda i, j: (0, j))],
        out_specs=[pl.BlockSpec((tm, tn), lambda i, j: (i, j)),
                   pl.BlockSpec((tm, LANES), lambda i, j: (i, 0))],
        out_shape=[jax.ShapeDtypeStruct((t, n), bf16),
                   jax.ShapeDtypeStruct((t, LANES), f32)],
        scratch_shapes=[pltpu.VMEM((tm, d), bf16)],
        compiler_params=_cparams(("arbitrary", "arbitrary")),
    )(x2, ada3, norm1_g.reshape(1, d), w_small, w_main)


def _softplus(x):
    return jnp.maximum(x, 0.0) + jnp.log1p(jnp.exp(-jnp.abs(x)))


def _ssd_kernel(z_ref, xs_ref, bc_ref, sm_ref, cw_ref, cb_ref, dtb_ref, alog_ref, dsk_ref, ng_ref, e_ref, sh_ref,
                o_ref, xx, cbuf, state, ybuf):
    q = SSM_CHUNK
    n_st = SSM_STATE
    gw = SSM_HPG * SSM_HEAD_DIM

    @pl.when(pl.program_id(1) == 0)
    def _():
        xx[0:q, :] = jnp.zeros((q, SSM_CONV_DIM), bf16)
        state[...] = jnp.zeros(state.shape, f32)

    xx[q:2 * q, 0:SSM_D_INNER] = xs_ref[...]
    xx[q:2 * q, SSM_D_INNER:SSM_CONV_DIM] = bc_ref[...]
    cblk = 512
    for j in range(SSM_CONV_DIM // cblk):
        cols = slice(j * cblk, (j + 1) * cblk)
        shifted = _dot(sh_ref[...], xx[:, cols])
        acc = cb_ref[:, cols] + cw_ref[SSM_CONV - 1:SSM_CONV, cols] * xx[q:2 * q, cols].astype(f32)
        for k in range(SSM_CONV - 1):
            acc = acc + cw_ref[k:k + 1, cols] * shifted[k * q:(k + 1) * q]
        cbuf[:, cols] = _silu(acc)
    xx[0:q, :] = xx[q:2 * q, :]

    lane = lax.broadcasted_iota(jnp.int32, (1, LANES), 1)
    dt = _softplus(sm_ref[...] + dtb_ref[...])
    a = jnp.where(lane < SSM_HEADS, -jnp.exp(alog_ref[...]), 0.0)
    da = dt * a
    ri = lax.broadcasted_iota(jnp.int32, (q, q), 0)
    ci = lax.broadcasted_iota(jnp.int32, (q, q), 1)
    tril = ci <= ri
    tril_b = jnp.where(tril, 1.0, 0.0).astype(bf16)
    acum = _dot_exact_rhs_left(tril_b, da)
    acum_t = acum.T
    a_last = acum[q - 1:q, :]
    exp_a = jnp.exp(acum)
    w_end = jnp.exp(a_last - acum) * dt
    dt_b = dt.astype(bf16)
    expa_b = exp_a.astype(bf16)
    wend_b = w_end.astype(bf16)
    elast8 = jnp.broadcast_to(jnp.exp(a_last), (8, LANES))
    lo_mask = lane < SSM_HEAD_DIM

    for g in range(SSM_GROUPS):
        gs = slice(g * gw, (g + 1) * gw)
        e_g = e_ref[:, gs]
        xc = cbuf[:, gs]
        xdt = (xc * _dot(dt_b, e_g)).astype(bf16)
        xw = (xc * _dot(wend_b, e_g)).astype(bf16)
        bg = cbuf[:, SSM_D_INNER + g * n_st:SSM_D_INNER + (g + 1) * n_st]
        cg = cbuf[:, SSM_D_INNER + (SSM_GROUPS + g) * n_st:SSM_D_INNER + (SSM_GROUPS + g + 1) * n_st]
        bg_b = bg.astype(bf16)
        cg_b = cg.astype(bf16)
        cb = _dot_nt(cg_b, bg_b)
        st = state[g]
        yoff = _dot(cg_b, st.astype(bf16)) * _dot(expa_b, e_g)
        for hp in range(SSM_HPG // 2):
            xpair = xdt[:, hp * LANES:(hp + 1) * LANES]
            acc = yoff[:, hp * LANES:(hp + 1) * LANES]
            for par in range(2):
                h = g * SSM_HPG + 2 * hp + par
                xh = jnp.where(lo_mask, xpair, 0.0) if par == 0 else jnp.where(lo_mask, 0.0, xpair)
                seg = acum[:, h:h + 1] - acum_t[h:h + 1, :]
                decay = jnp.exp(jnp.where(tril, seg, NEG))
                acc = acc + _dot((cb * decay).astype(bf16), xh.astype(bf16))
            ybuf[:, hp * LANES:(hp + 1) * LANES] = acc
        state[g] = st * _dot_exact_rhs(elast8, e_g)[0:1, :] + _dot(bg.T.astype(bf16), xw)

        z = z_ref[:, gs].astype(f32)
        yg = (ybuf[...] + dsk_ref[:, gs] * xc) * _silu(z)
        ms = jnp.mean(yg * yg, axis=-1, keepdims=True)
        o_ref[:, gs] = (yg * lax.rsqrt(ms + NORM_EPS) * ng_ref[:, gs]).astype(bf16)


def _dot_exact_rhs_left(l_bf16, x):
    hi, mid, lo = _split3(x)
    return _dot(l_bf16, hi) + _dot(l_bf16, mid) + _dot(l_bf16, lo)


def _ssd(main, small, conv_w, conv_b, dt_bias, a_log, d_skip, ssm_norm_g, batch, seq):
    t = main.shape[0]
    q = SSM_CHUNK
    nc = seq // q
    pad = LANES - SSM_HEADS
    dtb = jnp.pad(dt_bias, (0, pad)).reshape(1, LANES)
    alog = jnp.pad(a_log, (0, pad)).reshape(1, LANES)
    dsk = jnp.repeat(d_skip, SSM_HEAD_DIM).reshape(1, SSM_D_INNER)
    e_np = (np.arange(LANES)[:, None] == (np.arange(SSM_D_INNER)[None, :] // SSM_HEAD_DIM)).astype(np.float32)
    e_mat = jnp.asarray(e_np, bf16)
    tt = np.arange(q)
    sh_np = np.zeros(((SSM_CONV - 1) * q, 2 * q), np.float32)
    for k in range(SSM_CONV - 1):
        sh_np[k * q + tt, q + tt - (SSM_CONV - 1) + k] = 1.0
    sh_mat = jnp.asarray(sh_np, bf16)
    row = lambda b, c: (b * nc + c)
    const = lambda b, c: (0, 0)
    return pl.pallas_call(
        _ssd_kernel,
        grid=(batch, nc),
        in_specs=[pl.BlockSpec((q, SSM_D_INNER), lambda b, c: (row(b, c), C_Z // SSM_D_INNER)),
                  pl.BlockSpec((q, SSM_D_INNER), lambda b, c: (row(b, c), C_X // SSM_D_INNER)),
                  pl.BlockSpec((q, 1024), lambda b, c: (row(b, c), C_BC // 1024)),
                  pl.BlockSpec((q, LANES), lambda b, c: (row(b, c), 0)),
                  pl.BlockSpec((SSM_CONV, SSM_CONV_DIM), const),
                  pl.BlockSpec((1, SSM_CONV_DIM), const),
                  pl.BlockSpec((1, LANES), const),
                  pl.BlockSpec((1, LANES), const),
                  pl.BlockSpec((1, SSM_D_INNER), const),
                  pl.BlockSpec((1, SSM_D_INNER), const),
                  pl.BlockSpec((LANES, SSM_D_INNER), const),
                  pl.BlockSpec(((SSM_CONV - 1) * q, 2 * q), const)],
        out_specs=pl.BlockSpec((q, SSM_D_INNER), lambda b, c: (row(b, c), 0)),
        out_shape=jax.ShapeDtypeStruct((t, SSM_D_INNER), bf16),
        scratch_shapes=[pltpu.VMEM((2 * q, SSM_CONV_DIM), bf16),
                        pltpu.VMEM((q, SSM_CONV_DIM), f32),
                        pltpu.VMEM((SSM_GROUPS, SSM_STATE, SSM_HPG * SSM_HEAD_DIM), f32),
                        pltpu.VMEM((q, SSM_HPG * SSM_HEAD_DIM), f32)],
        compiler_params=_cparams(("arbitrary", "arbitrary")),
    )(main, main, main, small, conv_w, conv_b.reshape(1, -1), dtb, alog, dsk,
      ssm_norm_g.reshape(1, -1), e_mat, sh_mat)


def _nsa_kernel(q_ref, kc_ref, vc_ref, ks_ref, vs_ref, kw_ref, vw_ref, sm_ref, cos_ref, sin_ref,
                pek_ref, w1k_ref, w2k_ref, pev_ref, w1v_ref, w2v_ref, ovl_ref, eg_ref,
                o_ref,
                ks_p, vs_p, kw_p, vw_p, kc_buf, vc_buf, kcmp_p, vcmp_p, q_buf, c_buf, b_buf, s_buf, mx_buf, o_acc):
    g = pl.program_id(1)
    seq = q_ref.shape[0]
    qb = NSA_QTILE
    dk = NSA_HEAD_DIM
    n_cmp = seq // CMP_STRIDE
    n_pairs = NSA_HPG // 2
    rows = NSA_HPG * qb
    slc_rows = s_buf.shape[1]
    n_key_chunks = seq // KEY_CHUNK
    sel_shift = SEL_BLOCK.bit_length() - 1
    scale = dk ** -0.5

    lane = lax.broadcasted_iota(jnp.int32, (1, LANES), 1)
    lo = lane < dk
    first_half = (lane & (dk - 1)) < dk // 2

    def rope(tv, cs, sn):
        rot = jnp.where(first_half, pltpu.roll(tv, LANES - dk // 2, 1), pltpu.roll(tv, dk // 2, 1))
        return tv * cs + rot * sn

    def group_lanes(tv):
        return jnp.where(lo, jnp.where(g == 0, tv, pltpu.roll(tv, dk, 1)), 0.0)

    def with_ones(tv):
        return jnp.where(lane == dk, 1.0, tv)

    def normalise(acc):
        return jnp.where(lo, acc / acc[:, dk:dk + 1], 0.0)

    prep_rows = 256

    def prep(i, carry):
        r0 = pl.multiple_of(i * prep_rows, prep_rows)
        sl = pl.ds(r0, prep_rows)
        cs = cos_ref[sl, :]
        sn = sin_ref[sl, :]
        ks_p[sl, :] = group_lanes(rope(ks_ref[sl, :].astype(f32), cs, sn)).astype(bf16)
        vs_p[sl, :] = with_ones(group_lanes(vs_ref[sl, :].astype(f32))).astype(bf16)
        slw = pl.ds(pl.multiple_of(r0 + WIN_PAD, 64), prep_rows)
        kw_p[slw, :] = group_lanes(rope(kw_ref[sl, :].astype(f32), cs, sn)).astype(bf16)
        vw_p[slw, :] = group_lanes(vw_ref[sl, :].astype(f32)).astype(bf16)
        kc_buf[sl, :] = group_lanes(rope(kc_ref[sl, :].astype(f32), cs, sn))
        vc_buf[sl, :] = group_lanes(vc_ref[sl, :].astype(f32))
        return carry

    lax.fori_loop(0, seq // prep_rows, prep, 0)
    kw_p[0:WIN_PAD, :] = jnp.zeros((WIN_PAD, LANES), bf16)
    vw_p[0:WIN_PAD, :] = jnp.zeros((WIN_PAD, LANES), bf16)
    kc_buf[seq:seq + CMP_BLOCK, :] = jnp.zeros((CMP_BLOCK, LANES), f32)
    vc_buf[seq:seq + CMP_BLOCK, :] = jnp.zeros((CMP_BLOCK, LANES), f32)

    def compress(buf, pe_ref, w1_ref, w2_ref, out):
        pre = jnp.zeros((n_cmp, CMP_HIDDEN), f32)
        for l in range(CMP_BLOCK):
            xl = buf[pl.ds(l, n_cmp, stride=CMP_STRIDE), :] + pe_ref[l:l + 1, :]
            pre = pre + _dot(xl.astype(bf16), w1_ref[l * LANES:(l + 1) * LANES, :])
        out[...] = _dot(_silu(pre).astype(bf16), w2_ref[...]).astype(bf16)

    compress(kc_buf, pek_ref, w1k_ref, w2k_ref, kcmp_p)
    compress(vc_buf, pev_ref, w1v_ref, w2v_ref, vcmp_p)

    def tile8(x):
        return jnp.concatenate([x] * NSA_HPG, axis=0)

    wr = lax.broadcasted_iota(jnp.int32, (qb, WIN_KEYS), 0)
    wc = lax.broadcasted_iota(jnp.int32, (qb, WIN_KEYS), 1)

    def stage_a(ti, slot):
        s0 = pl.multiple_of(ti * qb, qb)
        sl = pl.ds(s0, qb)
        tq = s0 + lax.broadcasted_iota(jnp.int32, (qb, 1), 0)

        q2 = q_ref[sl, :].astype(f32)
        qs = jnp.concatenate([q2[:, p * LANES:(p + 1) * LANES] for p in range(n_pairs)], axis=0)
        cs = jnp.concatenate([cos_ref[sl, :]] * n_pairs, axis=0)
        sn = jnp.concatenate([sin_ref[sl, :]] * n_pairs, axis=0)
        qr = rope(qs, cs, sn) * scale
        q_even = jnp.where(lo, qr, 0.0)
        q_odd = jnp.where(lo, pltpu.roll(qr, dk, 1), 0.0)
        qall = jnp.concatenate([q_even, q_odd], axis=0).astype(bf16)

        sc = _dot_nt(qall, kcmp_p[...])
        ccol = lax.broadcasted_iota(jnp.int32, (qb, n_cmp), 1)
        cmask = (ccol * CMP_STRIDE + (CMP_BLOCK - 1)) <= tq
        cbias = tile8(jnp.where(cmask, 0.0, NEG))
        cmf = tile8(jnp.where(cmask, 1.0, 0.0))
        scb = sc + cbias
        pc = jnp.exp(scb - jnp.max(scb, axis=-1, keepdims=True)) * cmf
        pc = pc / jnp.maximum(jnp.sum(pc, axis=-1, keepdims=True), 1e-30)
        o_cmp = _dot(pc.astype(bf16), vcmp_p[...])
        pcs = pc[0:qb]
        for r in range(1, NSA_HPG):
            pcs = pcs + pc[r * qb:(r + 1) * qb]
        imp = _dot_exact_rhs(pcs, ovl_ref[...])

        jl = lax.broadcasted_iota(jnp.int32, (qb, LANES), 1)
        local = tq >> sel_shift
        forced = (jl == local) | (jl == 0)
        val = jnp.where(forced | (jl > local), -jnp.inf, imp)
        sel = jnp.where(forced, 1.0, 0.0)
        for _ in range(SEL_TOPK - 2):
            mx = jnp.max(val, axis=-1, keepdims=True)
            idx = jnp.min(jnp.where((val == mx) & (val > -jnp.inf), jl, LANES), axis=-1, keepdims=True)
            hit = jl == idx
            sel = jnp.where(hit, 1.0, sel)
            val = jnp.where(hit, -jnp.inf, val)
        sel_b = sel.astype(bf16)

        q_buf[slot] = qall
        c_buf[slot] = o_cmp
        er = lax.broadcasted_iota(jnp.int32, (LANES, KEY_CHUNK), 0)
        ec = lax.broadcasted_iota(jnp.int32, (LANES, KEY_CHUNK), 1)
        kr = lax.broadcasted_iota(jnp.int32, (qb, KEY_CHUNK), 1)
        for c in range(n_key_chunks):
            expand = jnp.where(er == c * (KEY_CHUNK // SEL_BLOCK) + (ec >> sel_shift), 1.0, 0.0).astype(bf16)
            ok_s = (_dot(sel_b, expand) > 0.5) & ((c * KEY_CHUNK + kr) <= tq)
            b_buf[slot, c] = jnp.where(ok_s, 0.0, NEG)

    half = n_pairs * qb

    def qblock(qi, carry):
        slot = 0
        stage_a(qi, slot)
        s0 = pl.multiple_of(qi * qb, qb)
        sl = pl.ds(s0, qb)
        qall = q_buf[slot]

        kk = kw_p[pl.ds(s0, WIN_KEYS), :]
        vv = vw_p[pl.ds(s0, WIN_KEYS), :]
        rel = wc - WIN_PAD - wr
        ok = (rel <= 0) & (rel > -WINDOW) & ((wc - WIN_PAD + s0) >= 0)
        wbias = jnp.concatenate([jnp.where(ok, 0.0, NEG)] * 2, axis=0)
        for hr in range(rows // (2 * qb)):
            rs = slice(hr * 2 * qb, (hr + 1) * 2 * qb)
            sw = _dot_nt(qall[rs], kk) + wbias
            pw = jnp.exp(sw - jnp.max(sw, axis=-1, keepdims=True))
            o_acc[1, rs, :] = _dot(pw.astype(bf16), vv) / jnp.sum(pw, axis=-1, keepdims=True)

        n_chunks = ((s0 + qb - 1) >> (KEY_CHUNK.bit_length() - 1)) + 1
        n_steps = (n_chunks + 1) >> 1
        mx_buf[...] = jnp.full((rows, LANES), NEG, f32)
        o_acc[0] = jnp.zeros((rows, LANES), f32)
        for hv in range(rows // slc_rows):
            rs = slice(hv * slc_rows, (hv + 1) * slc_rows)

            def pass_a(step, carry_a, rs=rs):
                mx = mx_buf[rs, :]
                for u in range(2):
                    c = 2 * step + u
                    k0 = pl.multiple_of(c * KEY_CHUNK, KEY_CHUNK)
                    bias = jnp.concatenate([b_buf[slot, c]] * (slc_rows // qb), axis=0)
                    s = _dot_nt(q_buf[slot, rs, :], ks_p[pl.ds(k0, KEY_CHUNK), :]) + bias
                    s_buf[c] = s
                    mx = jnp.maximum(mx, jnp.maximum(s[:, :LANES], s[:, LANES:]))
                mx_buf[rs, :] = mx
                return carry_a

            lax.fori_loop(0, n_steps, pass_a, 0)
            mx_buf[rs, :] = jnp.broadcast_to(jnp.max(mx_buf[rs, :], axis=-1, keepdims=True), (slc_rows, LANES))

            def pass_b(step, carry_b, rs=rs):
                mb = jnp.concatenate([mx_buf[rs, :]] * (KEY_CHUNK // LANES), axis=1)
                acc = o_acc[0, rs, :]
                for u in range(2):
                    c = 2 * step + u
                    k0 = pl.multiple_of(c * KEY_CHUNK, KEY_CHUNK)
                    p = jnp.exp((s_buf[c] - mb).astype(bf16))
                    acc = acc + _dot(p, vs_p[pl.ds(k0, KEY_CHUNK), :])
                o_acc[0, rs, :] = acc
                return carry_b

            lax.fori_loop(0, n_steps, pass_b, 0)
        o_cmp = c_buf[slot]
        o_slc = normalise(o_acc[0])
        o_win = o_acc[1]

        gates = _dot(jax.nn.sigmoid(sm_ref[sl, :]).astype(bf16), eg_ref[0])

        def pair_tile(o, p):
            return o[p * qb:(p + 1) * qb] + pltpu.roll(o[half + p * qb:half + (p + 1) * qb], dk, 1)

        width = NSA_HPG * dk
        for p in range(n_pairs):
            cols = slice(p * LANES, (p + 1) * LANES)
            out = (gates[:, 0 * width + p * LANES:0 * width + (p + 1) * LANES] * pair_tile(o_cmp, p)
                   + gates[:, 1 * width + p * LANES:1 * width + (p + 1) * LANES] * pair_tile(o_slc, p)
                   + gates[:, 2 * width + p * LANES:2 * width + (p + 1) * LANES] * pair_tile(o_win, p))
            o_ref[sl, cols] = out.astype(bf16)
        return carry

    lax.fori_loop(0, seq // qb, qblock, 0)


def _pad_w1(w1):
    h = w1.shape[1]
    w = w1.reshape(CMP_BLOCK, NSA_HEAD_DIM, h)
    return jnp.pad(w, ((0, 0), (0, LANES - NSA_HEAD_DIM), (0, 0))).reshape(CMP_BLOCK * LANES, h).astype(bf16)


def _nsa(main, small, cos_t, sin_t, cmp_pe_k, cmp_w1_k, cmp_w2_k, cmp_pe_v, cmp_w1_v, cmp_w2_v, batch, seq):
    t = main.shape[0]
    assert seq % (2 * KEY_CHUNK) == 0 and seq // SEL_BLOCK <= LANES
    dk = NSA_HEAD_DIM
    width = NSA_HPG * dk
    n_cmp = seq // CMP_STRIDE
    pad_l = ((0, 0), (0, LANES - dk))
    pek = jnp.pad(cmp_pe_k, pad_l)
    pev = jnp.pad(cmp_pe_v, pad_l)
    w1k = _pad_w1(cmp_w1_k)
    w1v = _pad_w1(cmp_w1_v)
    w2k = jnp.pad(cmp_w2_k, pad_l).astype(bf16)
    w2v = jnp.pad(cmp_w2_v, pad_l).astype(bf16)
    cstart = np.arange(LANES)[:, None] * CMP_STRIDE
    sstart = np.arange(LANES)[None, :] * SEL_BLOCK
    valid_c = np.arange(LANES)[:, None] < (seq - CMP_BLOCK) // CMP_STRIDE + 1
    valid_j = np.arange(LANES)[None, :] < seq // SEL_BLOCK
    ovl = ((cstart < sstart + SEL_BLOCK) & (cstart + CMP_BLOCK > sstart) & valid_c & valid_j).astype(np.float32)
    ovl = jnp.asarray(ovl[:n_cmp] if n_cmp <= LANES else ovl, bf16)
    eg = np.zeros((NSA_KV_GROUPS, LANES, 3 * width), np.float32)
    for gg in range(NSA_KV_GROUPS):
        for r in range(NSA_HPG):
            for j in range(3):
                eg[gg, SSM_HEADS + gg * NSA_HPG * 3 + r * 3 + j, j * width + r * dk:j * width + (r + 1) * dk] = 1.0
    eg = jnp.asarray(eg, bf16)

    kv_blk = C_KV // LANES
    kv_spec = lambda j: pl.BlockSpec((seq, LANES), lambda b, g: (b, kv_blk + j))
    const2 = lambda b, g: (0, 0)
    return pl.pallas_call(
        _nsa_kernel,
        grid=(batch, NSA_KV_GROUPS),
        in_specs=[pl.BlockSpec((seq, width), lambda b, g: (b, C_Q // width + g))]
                 + [kv_spec(j) for j in range(6)]
                 + [pl.BlockSpec((seq, LANES), lambda b, g: (b, 0))] * 3
                 + [pl.BlockSpec((CMP_BLOCK, LANES), const2),
                    pl.BlockSpec((CMP_BLOCK * LANES, CMP_HIDDEN), const2),
                    pl.BlockSpec((CMP_HIDDEN, LANES), const2)] * 2
                 + [pl.BlockSpec(ovl.shape, const2),
                    pl.BlockSpec((1, LANES, 3 * width), lambda b, g: (g, 0, 0))],
        out_specs=pl.BlockSpec((seq, width), lambda b, g: (b, g)),
        out_shape=jax.ShapeDtypeStruct((t, NSA_WIDTH), bf16),
        scratch_shapes=[pltpu.VMEM((seq, LANES), bf16),
                        pltpu.VMEM((seq, LANES), bf16),
                        pltpu.VMEM((seq + WIN_PAD, LANES), bf16),
                        pltpu.VMEM((seq + WIN_PAD, LANES), bf16),
                        pltpu.VMEM((seq + CMP_BLOCK, LANES), f32),
                        pltpu.VMEM((seq + CMP_BLOCK, LANES), f32),
                        pltpu.VMEM((n_cmp, LANES), bf16),
                        pltpu.VMEM((n_cmp, LANES), bf16),
                        pltpu.VMEM((1, NSA_HPG * NSA_QTILE, LANES), bf16),
                        pltpu.VMEM((1, NSA_HPG * NSA_QTILE, LANES), f32),
                        pltpu.VMEM((1, seq // KEY_CHUNK, NSA_QTILE, KEY_CHUNK), f32),
                        pltpu.VMEM((seq // KEY_CHUNK, min(NSA_HPG * NSA_QTILE, SLC_ROWS), KEY_CHUNK), f32),
                        pltpu.VMEM((NSA_HPG * NSA_QTILE, LANES), f32),
                        pltpu.VMEM((2, NSA_HPG * NSA_QTILE, LANES), f32)],
        compiler_params=_cparams(("arbitrary", "arbitrary")),
    )(main, main, main, main, main, main, main, small, cos_t, sin_t,
      pek, w1k, w2k, pev, w1v, w2v, ovl, eg)


def _post_kernel(yn_ref, on_ref, gs_ref, ga_ref, x_ref, ada_ref, wso_ref, wno_ref, wo_ref, g2_ref, wr_ref, br_ref,
                 h1_ref, hn2_ref, cw_ref):
    tm = x_ref.shape[0]
    y_ssm = _dot(yn_ref[...], wso_ref[...])
    y_nsa = _dot(on_ref[...], wno_ref[...])
    merged = jax.nn.sigmoid(gs_ref[...].astype(f32)) * y_ssm + jax.nn.sigmoid(ga_ref[...].astype(f32)) * y_nsa
    ada = ada_ref[0]
    h1 = x_ref[...] + ada[2:3] * _dot(merged.astype(bf16), wo_ref[...])
    h1_ref[...] = h1
    hn2 = h1 * lax.rsqrt(jnp.mean(h1 * h1, axis=-1, keepdims=True) + NORM_EPS) * g2_ref[...]
    hn2 = hn2 * (1.0 + ada[4:5]) + ada[3:4]
    hn2_ref[...] = hn2.astype(bf16)

    logits = _dot_hp(hn2, wr_ref[...]) + br_ref[...]
    lane = lax.broadcasted_iota(jnp.int32, (tm, LANES), 1)
    is_g = lane < MOE_GROUPS
    lg = jnp.where(is_g, logits, NEG)
    eg = jnp.where(is_g, jnp.exp(lg - jnp.max(lg, axis=-1, keepdims=True)), 0.0)
    pg_all = eg / jnp.sum(eg, axis=-1, keepdims=True)
    pg = jnp.max(pg_all, axis=-1, keepdims=True)
    gsel = jnp.min(jnp.where(is_g & (pg_all == pg), lane, LANES), axis=-1, keepdims=True)
    member = (lane >= MOE_GROUPS) & (lane < MOE_GROUPS + N_EXPERTS) & (((lane - MOE_GROUPS) >> 3) == gsel)
    le = jnp.where(member, logits, NEG)
    ee = jnp.where(member, jnp.exp(le - jnp.max(le, axis=-1, keepdims=True)), 0.0)
    pe = jnp.where(member, ee / jnp.sum(ee, axis=-1, keepdims=True), -1.0)
    v0 = jnp.max(pe, axis=-1, keepdims=True)
    i0 = jnp.min(jnp.where(pe == v0, lane, LANES), axis=-1, keepdims=True)
    hit0 = lane == i0
    pe1 = jnp.where(hit0, -1.0, pe)
    v1 = jnp.max(pe1, axis=-1, keepdims=True)
    i1 = jnp.min(jnp.where(pe1 == v1, lane, LANES), axis=-1, keepdims=True)
    hit1 = lane == i1
    den = v0 + v1
    w0 = pg * v0 / den
    w1 = pg * v1 / den

    cw_ref[...] = jnp.where(hit0, w0, jnp.where(hit1, w1, 0.0))


def _post(yn, o_nsa, main, x2, ada3, w_ssm_out, w_nsa_out, w_o, norm2_g, w_route, b_route, seq):
    t, d = x2.shape
    tm = 512
    const = lambda i: (0, 0)
    return pl.pallas_call(
        _post_kernel,
        grid=(t // tm,),
        in_specs=[pl.BlockSpec((tm, SSM_D_INNER), lambda i: (i, 0)),
                  pl.BlockSpec((tm, NSA_WIDTH), lambda i: (i, 0)),
                  pl.BlockSpec((tm, d), lambda i: (i, C_GS // d)),
                  pl.BlockSpec((tm, d), lambda i: (i, C_GA // d)),
                  pl.BlockSpec((tm, d), lambda i: (i, 0)),
                  pl.BlockSpec((1, 6, d), lambda i: ((i * tm) // seq, 0, 0)),
                  pl.BlockSpec((SSM_D_INNER, d), const),
                  pl.BlockSpec((NSA_WIDTH, d), const),
                  pl.BlockSpec((d, d), const),
                  pl.BlockSpec((1, d), const),
                  pl.BlockSpec((d, LANES), const),
                  pl.BlockSpec((1, LANES), const)],
        out_specs=[pl.BlockSpec((tm, d), lambda i: (i, 0)),
                   pl.BlockSpec((tm, d), lambda i: (i, 0)),
                   pl.BlockSpec((tm, LANES), lambda i: (i, 0))],
        out_shape=[jax.ShapeDtypeStruct((t, d), f32),
                   jax.ShapeDtypeStruct((t, d), bf16),
                   jax.ShapeDtypeStruct((t, LANES), f32)],
        compiler_params=_cparams(("arbitrary",)),
    )(yn, o_nsa, main, main, x2, ada3, w_ssm_out, w_nsa_out, w_o, norm2_g.reshape(1, d), w_route, b_route)


def _moe_kernel(h1_ref, x_ref, cw_ref, ada_ref, wgu_ref, wd_ref, fg_ref, o_ref, y_acc, hcat):
    g = pl.program_id(1)
    tm = x_ref.shape[0]
    hd = EXPERT_HIDDEN

    @pl.when(g == 0)
    def _():
        y_acc[...] = jnp.zeros(y_acc.shape, f32)

    x = x_ref[...]
    cw = cw_ref[...]
    lane = lax.broadcasted_iota(jnp.int32, (tm, LANES), 1)
    for e in range(EXPERTS_PER_GROUP):
        hgu = _dot(x, wgu_ref[0, :, 2 * e * hd:2 * (e + 1) * hd])
        ce = jnp.sum(jnp.where(lane == MOE_GROUPS + g * EXPERTS_PER_GROUP + e, cw, 0.0), axis=-1, keepdims=True)
        hcat[:, e * hd:(e + 1) * hd] = (_silu(hgu[:, :hd]) * hgu[:, hd:] * ce).astype(bf16)
    y_acc[...] += _dot(hcat[...], wd_ref[0])

    @pl.when(g == MOE_GROUPS - 1)
    def _():
        h2 = h1_ref[...] + ada_ref[0][5:6] * y_acc[...]
        o_ref[...] = h2 * lax.rsqrt(jnp.mean(h2 * h2, axis=-1, keepdims=True) + NORM_EPS) * fg_ref[...]


def _moe(h1, hn2, cw, ada3, w_gate, w_up, w_down, final_g, seq):
    t, d = h1.shape
    tm = 512
    hd = EXPERT_HIDDEN
    epg = EXPERTS_PER_GROUP
    wgu = jnp.concatenate([w_gate, w_up], axis=-1).astype(bf16).reshape(MOE_GROUPS, epg, d, 2 * hd)
    wgu = wgu.transpose(0, 2, 1, 3).reshape(MOE_GROUPS, d, epg * 2 * hd)
    wd = w_down.astype(bf16).reshape(MOE_GROUPS, epg * hd, d)
    return pl.pallas_call(
        _moe_kernel,
        grid=(t // tm, MOE_GROUPS),
        in_specs=[pl.BlockSpec((tm, d), lambda i, g: (i, 0)),
                  pl.BlockSpec((tm, d), lambda i, g: (i, 0)),
                  pl.BlockSpec((tm, LANES), lambda i, g: (i, 0)),
                  pl.BlockSpec((1, 6, d), lambda i, g: ((i * tm) // seq, 0, 0)),
                  pl.BlockSpec((1, d, epg * 2 * hd), lambda i, g: (g, 0, 0)),
                  pl.BlockSpec((1, epg * hd, d), lambda i, g: (g, 0, 0)),
                  pl.BlockSpec((1, d), lambda i, g: (0, 0))],
        out_specs=pl.BlockSpec((tm, d), lambda i, g: (i, 0)),
        out_shape=jax.ShapeDtypeStruct((t, d), f32),
        scratch_shapes=[pltpu.VMEM((tm, d), f32),
                        pltpu.VMEM((tm, epg * hd), bf16)],
        compiler_params=_cparams(("arbitrary", "arbitrary")),
    )(h1, hn2, cw, ada3, wgu, wd, final_g.reshape(1, d))


def _pack_w_in(w_in):
    o = np.cumsum([0, SSM_D_INNER, SSM_CONV_DIM, SSM_HEADS, NSA_WIDTH] + [NSA_KV_WIDTH] * 6
                  + [3 * NSA_HEADS, D_MODEL, D_MODEL])
    z_xbc = w_in[:, o[0]:o[2]]
    w_dt = w_in[:, o[2]:o[3]]
    w_q = w_in[:, o[3]:o[4]]
    w_kv = w_in[:, o[4]:o[10]]
    w_gn = w_in[:, o[10]:o[11]]
    w_gsa = w_in[:, o[11]:o[13]]
    w_main = jnp.concatenate([z_xbc, w_q, w_gsa, w_kv], axis=1).astype(bf16)
    w_small = jnp.concatenate(
        [w_dt, w_gn, jnp.zeros((w_in.shape[0], LANES - SSM_HEADS - 3 * NSA_HEADS), f32)], axis=1)
    return w_main, w_small


def _layer(h, ada3, cos_t, sin_t, batch, seq, norm1_g, w_in, conv_w, conv_b, dt_bias, a_log, d_skip, ssm_norm_g,
           w_ssm_out, cmp_pe_k, cmp_w1_k, cmp_w2_k, cmp_pe_v, cmp_w1_v, cmp_w2_v, w_nsa_out, w_o, norm2_g,
           w_router_group, b_router_group, w_router_expert, b_router_expert, w_exp_gate, w_exp_up, w_exp_down,
           final_g):
    t, d = h.shape
    w_main, w_small = _pack_w_in(w_in)
    main, small = _inproj(h, ada3, norm1_g, w_small, w_main, seq)
    yn = _ssd(main, small, conv_w, conv_b, dt_bias, a_log, d_skip, ssm_norm_g, batch, seq)
    o_nsa = _nsa(main, small, cos_t, sin_t, cmp_pe_k, cmp_w1_k, cmp_w2_k, cmp_pe_v, cmp_w1_v, cmp_w2_v, batch, seq)
    rpad = LANES - MOE_GROUPS - N_EXPERTS
    w_route = jnp.concatenate([w_router_group, w_router_expert, jnp.zeros((d, rpad), f32)], axis=1)
    b_route = jnp.concatenate([b_router_group, b_router_expert, jnp.zeros((rpad,), f32)]).reshape(1, LANES)
    h1, hn2, cw = _post(yn, o_nsa, main, h, ada3, w_ssm_out.astype(bf16), w_nsa_out.astype(bf16),
                        w_o.astype(bf16), norm2_g, w_route, b_route, seq)
    return _moe(h1, hn2, cw, ada3, w_exp_gate, w_exp_up, w_exp_down, final_g, seq)


def kernel(x, c, positions, w_ada, b_ada, norm1_g, w_in, conv_w, conv_b, dt_bias, a_log, d_skip, ssm_norm_g, w_ssm_out, cmp_pe_k, cmp_w1_k, cmp_w2_k, cmp_pe_v, cmp_w1_v, cmp_w2_v, w_nsa_out, w_o, norm2_g, w_router_group, b_router_group, w_router_expert, b_router_expert, w_exp_gate, w_exp_up, w_exp_down, final_g):
    batch, seq, d = x.shape
    depth = w_ada.shape[0]
    assert depth == 1, "final RMSNorm is fused into the single layer's MoE combine"
    cos_t, sin_t = _rope_tables(positions)
    h = x.reshape(batch * seq, d)
    l = 0
    ada3 = _ada(c, w_ada[l], b_ada[l]).reshape(batch, 6, d)
    out = _layer(h, ada3, cos_t, sin_t, batch, seq, norm1_g[l], w_in[l], conv_w[l], conv_b[l], dt_bias[l],
                 a_log[l], d_skip[l], ssm_norm_g[l], w_ssm_out[l], cmp_pe_k[l], cmp_w1_k[l], cmp_w2_k[l],
                 cmp_pe_v[l], cmp_w1_v[l], cmp_w2_v[l], w_nsa_out[l], w_o[l], norm2_g[l], w_router_group[l],
                 b_router_group[l], w_router_expert[l], b_router_expert[l], w_exp_gate[l], w_exp_up[l],
                 w_exp_down[l], final_g)
    return out.reshape(batch, seq, d)
```

```python
import functools
import math

import numpy as np
import jax
import jax.numpy as jnp
from jax import lax
from jax.experimental import pallas as pl
from jax.experimental.pallas import tpu as pltpu

D_MODEL = 1024
SSM_D_INNER = 2048
SSM_HEAD_DIM = 64
SSM_HEADS = 32
SSM_GROUPS = 4
SSM_HPG = 8
SSM_STATE = 128
SSM_CONV = 4
SSM_CHUNK = 128
SSM_CONV_DIM = 3072
NSA_HEADS = 16
NSA_HEAD_DIM = 64
NSA_WIDTH = 1024
NSA_KV_GROUPS = 2
NSA_HPG = 8
NSA_KV_WIDTH = 128
CMP_BLOCK = 32
CMP_STRIDE = 16
CMP_HIDDEN = 256
SEL_BLOCK = 64
SEL_TOPK = 4
WINDOW = 512
NSA_QBLOCK = 64
ROPE_THETA = 10000.0
MOE_GROUPS = 4
EXPERTS_PER_GROUP = 8
N_EXPERTS = 32
EXPERT_TOPK = 2
EXPERT_HIDDEN = 256
NORM_EPS = 1e-6

LANES = 128
NEG = -1e30
NSA_QTILE = 256
SLC_ROWS = 2048
WIN_KEYS = WINDOW + NSA_QTILE
WIN_PAD = WINDOW
KEY_CHUNK = 256
VMEM_LIMIT = 58 * 1024 * 1024

C_Z, C_X, C_BC, C_Q, C_GS, C_GA, C_KV, C_MAIN = 0, 2048, 4096, 5120, 6144, 7168, 8192, 8960

f32 = jnp.float32
bf16 = jnp.bfloat16


def _cparams(sem):
    return pltpu.CompilerParams(dimension_semantics=sem, vmem_limit_bytes=VMEM_LIMIT)


def _dot(a, b):
    return jnp.dot(a, b, preferred_element_type=f32)


def _dot_nt(a, b):
    return lax.dot_general(a, b, (((1,), (1,)), ((), ())), preferred_element_type=f32)


def _split3(x):
    hi = x.astype(bf16)
    r = x - hi.astype(f32)
    mid = r.astype(bf16)
    lo = (r - mid.astype(f32)).astype(bf16)
    return hi, mid, lo


def _dot_exact_rhs(a, b_bf16):
    hi, mid, lo = _split3(a)
    return _dot(hi, b_bf16) + _dot(mid, b_bf16) + _dot(lo, b_bf16)


def _dot_hp(a, b):
    ah = a.astype(bf16)
    al = (a - ah.astype(f32)).astype(bf16)
    bh = b.astype(bf16)
    bl = (b - bh.astype(f32)).astype(bf16)
    return _dot(ah, bh) + _dot(al, bh) + _dot(ah, bl)


def _silu(x):
    return x * jax.nn.sigmoid(x)


def _ada_kernel(c_ref, w_ref, b_ref, o_ref):
    o_ref[...] = _dot_hp(_silu(c_ref[...]), w_ref[...]) + b_ref[...]


def _ada(c, w_ada, b_ada):
    b, d = c.shape
    n = w_ada.shape[1]
    tn = 1024
    return pl.pallas_call(
        _ada_kernel,
        grid=(n // tn,),
        in_specs=[pl.BlockSpec((b, d), lambda j: (0, 0)),
                  pl.BlockSpec((d, tn), lambda j: (0, j)),
                  pl.BlockSpec((1, tn), lambda j: (0, j))],
        out_specs=pl.BlockSpec((b, tn), lambda j: (0, j)),
        out_shape=jax.ShapeDtypeStruct((b, n), f32),
        compiler_params=_cparams(("arbitrary",)),
    )(c, w_ada, b_ada.reshape(1, n))


def _rope_kernel(pos_ref, inv_ref, sgn_ref, cos_ref, sin_ref):
    ang = pos_ref[...].astype(f32) * inv_ref[...]
    cos_ref[...] = jnp.cos(ang)
    sin_ref[...] = jnp.sin(ang) * sgn_ref[...]


def _rope_tables(positions):
    t = positions.size
    tm = 2048
    inv = ROPE_THETA ** (-jnp.arange(0, NSA_HEAD_DIM, 2, dtype=f32) / NSA_HEAD_DIM)
    inv_full = jnp.tile(inv, LANES // 32).reshape(1, LANES)
    lane = np.arange(LANES)
    sgn = jnp.asarray(np.where((lane % 64) < 32, -1.0, 1.0), f32).reshape(1, LANES)
    return pl.pallas_call(
        _rope_kernel,
        grid=(t // tm,),
        in_specs=[pl.BlockSpec((tm, 1), lambda i: (i, 0)),
                  pl.BlockSpec((1, LANES), lambda i: (0, 0)),
                  pl.BlockSpec((1, LANES), lambda i: (0, 0))],
        out_specs=[pl.BlockSpec((tm, LANES), lambda i: (i, 0))] * 2,
        out_shape=[jax.ShapeDtypeStruct((t, LANES), f32)] * 2,
        compiler_params=_cparams(("arbitrary",)),
    )(positions.reshape(t, 1), inv_full, sgn)


def _inproj_kernel(x_ref, ada_ref, g_ref, ws_ref, w_ref, main_ref, small_ref, hn_buf):
    @pl.when(pl.program_id(1) == 0)
    def _():
        x = x_ref[...]
        y = x * lax.rsqrt(jnp.mean(x * x, axis=-1, keepdims=True) + NORM_EPS) * g_ref[...]
        ada = ada_ref[0]
        hn = y * (1.0 + ada[1:2]) + ada[0:1]
        hn_buf[...] = hn.astype(bf16)
        small_ref[...] = _dot_hp(hn, ws_ref[...])

    main_ref[...] = _dot(hn_buf[...], w_ref[...]).astype(main_ref.dtype)


def _inproj(x2, ada3, norm1_g, w_small, w_main, seq):
    t, d = x2.shape
    n = w_main.shape[1]
    tm = 1024
    tn = 1792
    return pl.pallas_call(
        _inproj_kernel,
        grid=(t // tm, n // tn),
        in_specs=[pl.BlockSpec((tm, d), lambda i, j: (i, 0)),
                  pl.BlockSpec((1, 6, d), lambda i, j: ((i * tm) // seq, 0, 0)),
                  pl.BlockSpec((1, d), lambda i, j: (0, 0)),
                  pl.BlockSpec((d, LANES), lambda i, j: (0, 0)),
                  pl.BlockSpec((d, tn), lambda i, j: (0, j))],
        out_specs=[pl.BlockSpec((tm, tn), lambda i, j: (i, j)),
                   pl.BlockSpec((tm, LANES), lambda i, j: (i, 0))],
        out_shape=[jax.ShapeDtypeStruct((t, n), bf16),
                   jax.ShapeDtypeStruct((t, LANES), f32)],
        scratch_shapes=[pltpu.VMEM((tm, d), bf16)],
        compiler_params=_cparams(("arbitrary", "arbitrary")),
    )(x2, ada3, norm1_g.reshape(1, d), w_small, w_main)


def _softplus(x):
    return jnp.maximum(x, 0.0) + jnp.log1p(jnp.exp(-jnp.abs(x)))


def _ssd_kernel(z_ref, xs_ref, bc_ref, sm_ref, cw_ref, cb_ref, dtb_ref, alog_ref, dsk_ref, ng_ref, e_ref, sh_ref,
                o_ref, xx, cbuf, state, ybuf):
    q = SSM_CHUNK
    n_st = SSM_STATE
    gw = SSM_HPG * SSM_HEAD_DIM

    @pl.when(pl.program_id(1) == 0)
    def _():
        xx[0:q, :] = jnp.zeros((q, SSM_CONV_DIM), bf16)
        state[...] = jnp.zeros(state.shape, f32)

    xx[q:2 * q, 0:SSM_D_INNER] = xs_ref[...]
    xx[q:2 * q, SSM_D_INNER:SSM_CONV_DIM] = bc_ref[...]
    cblk = 512
    for j in range(SSM_CONV_DIM // cblk):
        cols = slice(j * cblk, (j + 1) * cblk)
        shifted = _dot(sh_ref[...], xx[:, cols])
        acc = cb_ref[:, cols] + cw_ref[SSM_CONV - 1:SSM_CONV, cols] * xx[q:2 * q, cols].astype(f32)
        for k in range(SSM_CONV - 1):
            acc = acc + cw_ref[k:k + 1, cols] * shifted[k * q:(k + 1) * q]
        cbuf[:, cols] = _silu(acc)
    xx[0:q, :] = xx[q:2 * q, :]

    lane = lax.broadcasted_iota(jnp.int32, (1, LANES), 1)
    dt = _softplus(sm_ref[...] + dtb_ref[...])
    a = jnp.where(lane < SSM_HEADS, -jnp.exp(alog_ref[...]), 0.0)
    da = dt * a
    ri = lax.broadcasted_iota(jnp.int32, (q, q), 0)
    ci = lax.broadcasted_iota(jnp.int32, (q, q), 1)
    tril = ci <= ri
    tril_b = jnp.where(tril, 1.0, 0.0).astype(bf16)
    acum = _dot_exact_rhs_left(tril_b, da)
    acum_t = acum.T
    a_last = acum[q - 1:q, :]
    exp_a = jnp.exp(acum)
    w_end = jnp.exp(a_last - acum) * dt
    dt_b = dt.astype(bf16)
    expa_b = exp_a.astype(bf16)
    wend_b = w_end.astype(bf16)
    elast8 = jnp.broadcast_to(jnp.exp(a_last), (8, LANES))
    lo_mask = lane < SSM_HEAD_DIM

    for g in range(SSM_GROUPS):
        gs = slice(g * gw, (g + 1) * gw)
        e_g = e_ref[:, gs]
        xc = cbuf[:, gs]
        xdt = (xc * _dot(dt_b, e_g)).astype(bf16)
        xw = (xc * _dot(wend_b, e_g)).astype(bf16)
        bg = cbuf[:, SSM_D_INNER + g * n_st:SSM_D_INNER + (g + 1) * n_st]
        cg = cbuf[:, SSM_D_INNER + (SSM_GROUPS + g) * n_st:SSM_D_INNER + (SSM_GROUPS + g + 1) * n_st]
        bg_b = bg.astype(bf16)
        cg_b = cg.astype(bf16)
        cb = _dot_nt(cg_b, bg_b)
        st = state[g]
        yoff = _dot(cg_b, st.astype(bf16)) * _dot(expa_b, e_g)
        for hp in range(SSM_HPG // 2):
            xpair = xdt[:, hp * LANES:(hp + 1) * LANES]
            acc = yoff[:, hp * LANES:(hp + 1) * LANES]
            for par in range(2):
                h = g * SSM_HPG + 2 * hp + par
                xh = jnp.where(lo_mask, xpair, 0.0) if par == 0 else jnp.where(lo_mask, 0.0, xpair)
                seg = acum[:, h:h + 1] - acum_t[h:h + 1, :]
                decay = jnp.exp(jnp.where(tril, seg, NEG))
                acc = acc + _dot((cb * decay).astype(bf16), xh.astype(bf16))
            ybuf[:, hp * LANES:(hp + 1) * LANES] = acc
        state[g] = st * _dot_exact_rhs(elast8, e_g)[0:1, :] + _dot(bg.T.astype(bf16), xw)

        z = z_ref[:, gs].astype(f32)
        yg = (ybuf[...] + dsk_ref[:, gs] * xc) * _silu(z)
        ms = jnp.mean(yg * yg, axis=-1, keepdims=True)
        o_ref[:, gs] = (yg * lax.rsqrt(ms + NORM_EPS) * ng_ref[:, gs]).astype(bf16)


def _dot_exact_rhs_left(l_bf16, x):
    hi, mid, lo = _split3(x)
    return _dot(l_bf16, hi) + _dot(l_bf16, mid) + _dot(l_bf16, lo)


def _ssd(main, small, conv_w, conv_b, dt_bias, a_log, d_skip, ssm_norm_g, batch, seq):
    t = main.shape[0]
    q = SSM_CHUNK
    nc = seq // q
    pad = LANES - SSM_HEADS
    dtb = jnp.pad(dt_bias, (0, pad)).reshape(1, LANES)
    alog = jnp.pad(a_log, (0, pad)).reshape(1, LANES)
    dsk = jnp.repeat(d_skip, SSM_HEAD_DIM).reshape(1, SSM_D_INNER)
    e_np = (np.arange(LANES)[:, None] == (np.arange(SSM_D_INNER)[None, :] // SSM_HEAD_DIM)).astype(np.float32)
    e_mat = jnp.asarray(e_np, bf16)
    tt = np.arange(q)
    sh_np = np.zeros(((SSM_CONV - 1) * q, 2 * q), np.float32)
    for k in range(SSM_CONV - 1):
        sh_np[k * q + tt, q + tt - (SSM_CONV - 1) + k] = 1.0
    sh_mat = jnp.asarray(sh_np, bf16)
    row = lambda b, c: (b * nc + c)
    const = lambda b, c: (0, 0)
    return pl.pallas_call(
        _ssd_kernel,
        grid=(batch, nc),
        in_specs=[pl.BlockSpec((q, SSM_D_INNER), lambda b, c: (row(b, c), C_Z // SSM_D_INNER)),
                  pl.BlockSpec((q, SSM_D_INNER), lambda b, c: (row(b, c), C_X // SSM_D_INNER)),
                  pl.BlockSpec((q, 1024), lambda b, c: (row(b, c), C_BC // 1024)),
                  pl.BlockSpec((q, LANES), lambda b, c: (row(b, c), 0)),
                  pl.BlockSpec((SSM_CONV, SSM_CONV_DIM), const),
                  pl.BlockSpec((1, SSM_CONV_DIM), const),
                  pl.BlockSpec((1, LANES), const),
                  pl.BlockSpec((1, LANES), const),
                  pl.BlockSpec((1, SSM_D_INNER), const),
                  pl.BlockSpec((1, SSM_D_INNER), const),
                  pl.BlockSpec((LANES, SSM_D_INNER), const),
                  pl.BlockSpec(((SSM_CONV - 1) * q, 2 * q), const)],
        out_specs=pl.BlockSpec((q, SSM_D_INNER), lambda b, c: (row(b, c), 0)),
        out_shape=jax.ShapeDtypeStruct((t, SSM_D_INNER), bf16),
        scratch_shapes=[pltpu.VMEM((2 * q, SSM_CONV_DIM), bf16),
                        pltpu.VMEM((q, SSM_CONV_DIM), f32),
                        pltpu.VMEM((SSM_GROUPS, SSM_STATE, SSM_HPG * SSM_HEAD_DIM), f32),
                        pltpu.VMEM((q, SSM_HPG * SSM_HEAD_DIM), f32)],
        compiler_params=_cparams(("arbitrary", "arbitrary")),
    )(main, main, main, small, conv_w, conv_b.reshape(1, -1), dtb, alog, dsk,
      ssm_norm_g.reshape(1, -1), e_mat, sh_mat)


def _nsa_kernel(q_ref, kc_ref, vc_ref, ks_ref, vs_ref, kw_ref, vw_ref, sm_ref, cos_ref, sin_ref,
                pek_ref, w1k_ref, w2k_ref, pev_ref, w1v_ref, w2v_ref, ovl_ref, eg_ref,
                o_ref,
                ks_p, vs_p, kw_p, vw_p, kc_buf, vc_buf, kcmp_p, vcmp_p, q_buf, c_buf, b_buf, s_buf, mx_buf, o_acc):
    g = pl.program_id(1)
    seq = q_ref.shape[0]
    qb = NSA_QTILE
    dk = NSA_HEAD_DIM
    n_cmp = seq // CMP_STRIDE
    n_pairs = NSA_HPG // 2
    rows = NSA_HPG * qb
    slc_rows = s_buf.shape[1]
    n_key_chunks = seq // KEY_CHUNK
    sel_shift = SEL_BLOCK.bit_length() - 1
    scale = dk ** -0.5

    lane = lax.broadcasted_iota(jnp.int32, (1, LANES), 1)
    lo = lane < dk
    first_half = (lane & (dk - 1)) < dk // 2

    def rope(tv, cs, sn):
        rot = jnp.where(first_half, pltpu.roll(tv, LANES - dk // 2, 1), pltpu.roll(tv, dk // 2, 1))
        return tv * cs + rot * sn

    def group_lanes(tv):
        return jnp.where(lo, jnp.where(g == 0, tv, pltpu.roll(tv, dk, 1)), 0.0)

    def with_ones(tv):
        return jnp.where(lane == dk, 1.0, tv)

    def normalise(acc):
        return jnp.where(lo, acc / acc[:, dk:dk + 1], 0.0)

    prep_rows = 256

    def prep(i, carry):
        r0 = pl.multiple_of(i * prep_rows, prep_rows)
        sl = pl.ds(r0, prep_rows)
        cs = cos_ref[sl, :]
        sn = sin_ref[sl, :]
        ks_p[sl, :] = group_lanes(rope(ks_ref[sl, :].astype(f32), cs, sn)).astype(bf16)
        vs_p[sl, :] = with_ones(group_lanes(vs_ref[sl, :].astype(f32))).astype(bf16)
        slw = pl.ds(pl.multiple_of(r0 + WIN_PAD, 64), prep_rows)
        kw_p[slw, :] = group_lanes(rope(kw_ref[sl, :].astype(f32), cs, sn)).astype(bf16)
        vw_p[slw, :] = group_lanes(vw_ref[sl, :].astype(f32)).astype(bf16)
        kc_buf[sl, :] = group_lanes(rope(kc_ref[sl, :].astype(f32), cs, sn))
        vc_buf[sl, :] = group_lanes(vc_ref[sl, :].astype(f32))
        return carry

    lax.fori_loop(0, seq // prep_rows, prep, 0)
    kw_p[0:WIN_PAD, :] = jnp.zeros((WIN_PAD, LANES), bf16)
    vw_p[0:WIN_PAD, :] = jnp.zeros((WIN_PAD, LANES), bf16)
    kc_buf[seq:seq + CMP_BLOCK, :] = jnp.zeros((CMP_BLOCK, LANES), f32)
    vc_buf[seq:seq + CMP_BLOCK, :] = jnp.zeros((CMP_BLOCK, LANES), f32)

    def compress(buf, pe_ref, w1_ref, w2_ref, out):
        pre = jnp.zeros((n_cmp, CMP_HIDDEN), f32)
        for l in range(CMP_BLOCK):
            xl = buf[pl.ds(l, n_cmp, stride=CMP_STRIDE), :] + pe_ref[l:l + 1, :]
            pre = pre + _dot(xl.astype(bf16), w1_ref[l * LANES:(l + 1) * LANES, :])
        out[...] = _dot(_silu(pre).astype(bf16), w2_ref[...]).astype(bf16)

    compress(kc_buf, pek_ref, w1k_ref, w2k_ref, kcmp_p)
    compress(vc_buf, pev_ref, w1v_ref, w2v_ref, vcmp_p)

    def tile8(x):
        return jnp.concatenate([x] * NSA_HPG, axis=0)

    wr = lax.broadcasted_iota(jnp.int32, (qb, WIN_KEYS), 0)
    wc = lax.broadcasted_iota(jnp.int32, (qb, WIN_KEYS), 1)

    def stage_a(ti, slot):
        s0 = pl.multiple_of(ti * qb, qb)
        sl = pl.ds(s0, qb)
        tq = s0 + lax.broadcasted_iota(jnp.int32, (qb, 1), 0)

        q2 = q_ref[sl, :].astype(f32)
        qs = jnp.concatenate([q2[:, p * LANES:(p + 1) * LANES] for p in range(n_pairs)], axis=0)
        cs = jnp.concatenate([cos_ref[sl, :]] * n_pairs, axis=0)
        sn = jnp.concatenate([sin_ref[sl, :]] * n_pairs, axis=0)
        qr = rope(qs, cs, sn) * scale
        q_even = jnp.where(lo, qr, 0.0)
        q_odd = jnp.where(lo, pltpu.roll(qr, dk, 1), 0.0)
        qall = jnp.concatenate([q_even, q_odd], axis=0).astype(bf16)

        sc = _dot_nt(qall, kcmp_p[...])
        ccol = lax.broadcasted_iota(jnp.int32, (qb, n_cmp), 1)
        cmask = (ccol * CMP_STRIDE + (CMP_BLOCK - 1)) <= tq
        cbias = tile8(jnp.where(cmask, 0.0, NEG))
        cmf = tile8(jnp.where(cmask, 1.0, 0.0))
        scb = sc + cbias
        pc = jnp.exp(scb - jnp.max(scb, axis=-1, keepdims=True)) * cmf
        pc = pc / jnp.maximum(jnp.sum(pc, axis=-1, keepdims=True), 1e-30)
        o_cmp = _dot(pc.astype(bf16), vcmp_p[...])
        pcs = pc[0:qb]
        for r in range(1, NSA_HPG):
            pcs = pcs + pc[r * qb:(r + 1) * qb]
        imp = _dot_exact_rhs(pcs, ovl_ref[...])

        jl = lax.broadcasted_iota(jnp.int32, (qb, LANES), 1)
        local = tq >> sel_shift
        forced = (jl == local) | (jl == 0)
        val = jnp.where(forced | (jl > local), -jnp.inf, imp)
        sel = jnp.where(forced, 1.0, 0.0)
        for _ in range(SEL_TOPK - 2):
            mx = jnp.max(val, axis=-1, keepdims=True)
            idx = jnp.min(jnp.where((val == mx) & (val > -jnp.inf), jl, LANES), axis=-1, keepdims=True)
            hit = jl == idx
            sel = jnp.where(hit, 1.0, sel)
            val = jnp.where(hit, -jnp.inf, val)
        sel_b = sel.astype(bf16)

        q_buf[slot] = qall
        c_buf[slot] = o_cmp
        er = lax.broadcasted_iota(jnp.int32, (LANES, KEY_CHUNK), 0)
        ec = lax.broadcasted_iota(jnp.int32, (LANES, KEY_CHUNK), 1)
        kr = lax.broadcasted_iota(jnp.int32, (qb, KEY_CHUNK), 1)
        for c in range(n_key_chunks):
            expand = jnp.where(er == c * (KEY_CHUNK // SEL_BLOCK) + (ec >> sel_shift), 1.0, 0.0).astype(bf16)
            ok_s = (_dot(sel_b, expand) > 0.5) & ((c * KEY_CHUNK + kr) <= tq)
            b_buf[slot, c] = jnp.where(ok_s, 0.0, NEG)

    half = n_pairs * qb

    def qblock(qi, carry):
        slot = 0
        stage_a(qi, slot)
        s0 = pl.multiple_of(qi * qb, qb)
        sl = pl.ds(s0, qb)
        qall = q_buf[slot]

        kk = kw_p[pl.ds(s0, WIN_KEYS), :]
        vv = vw_p[pl.ds(s0, WIN_KEYS), :]
        rel = wc - WIN_PAD - wr
        ok = (rel <= 0) & (rel > -WINDOW) & ((wc - WIN_PAD + s0) >= 0)
        wbias = jnp.concatenate([jnp.where(ok, 0.0, NEG)] * 2, axis=0)
        for hr in range(rows // (2 * qb)):
            rs = slice(hr * 2 * qb, (hr + 1) * 2 * qb)
            sw = _dot_nt(qall[rs], kk) + wbias
            pw = jnp.exp(sw - jnp.max(sw, axis=-1, keepdims=True))
            o_acc[1, rs, :] = _dot(pw.astype(bf16), vv) / jnp.sum(pw, axis=-1, keepdims=True)

        n_chunks = ((s0 + qb - 1) >> (KEY_CHUNK.bit_length() - 1)) + 1
        n_steps = (n_chunks + 1) >> 1
        mx_buf[...] = jnp.full((rows, LANES), NEG, f32)
        o_acc[0] = jnp.zeros((rows, LANES), f32)
        for hv in range(rows // slc_rows):
            rs = slice(hv * slc_rows, (hv + 1) * slc_rows)

            def pass_a(step, carry_a, rs=rs):
                mx = mx_buf[rs, :]
                for u in range(2):
                    c = 2 * step + u
                    k0 = pl.multiple_of(c * KEY_CHUNK, KEY_CHUNK)
                    bias = jnp.concatenate([b_buf[slot, c]] * (slc_rows // qb), axis=0)
                    s = _dot_nt(q_buf[slot, rs, :], ks_p[pl.ds(k0, KEY_CHUNK), :]) + bias
                    s_buf[c] = s
                    mx = jnp.maximum(mx, jnp.maximum(s[:, :LANES], s[:, LANES:]))
                mx_buf[rs, :] = mx
                return carry_a

            lax.fori_loop(0, n_steps, pass_a, 0)
            mx_buf[rs, :] = jnp.broadcast_to(jnp.max(mx_buf[rs, :], axis=-1, keepdims=True), (slc_rows, LANES))

            def pass_b(step, carry_b, rs=rs):
                mb = jnp.concatenate([mx_buf[rs, :]] * (KEY_CHUNK // LANES), axis=1)
                acc = o_acc[0, rs, :]
                for u in range(2):
                    c = 2 * step + u
                    k0 = pl.multiple_of(c * KEY_CHUNK, KEY_CHUNK)
                    p = jnp.exp((s_buf[c] - mb).astype(bf16))
                    acc = acc + _dot(p, vs_p[pl.ds(k0, KEY_CHUNK), :])
                o_acc[0, rs, :] = acc
                return carry_b

            lax.fori_loop(0, n_steps, pass_b, 0)
        o_cmp = c_buf[slot]
        o_slc = normalise(o_acc[0])
        o_win = o_acc[1]

        gates = _dot(jax.nn.sigmoid(sm_ref[sl, :]).astype(bf16), eg_ref[0])

        def pair_tile(o, p):
            return o[p * qb:(p + 1) * qb] + pltpu.roll(o[half + p * qb:half + (p + 1) * qb], dk, 1)

        width = NSA_HPG * dk
        for p in range(n_pairs):
            cols = slice(p * LANES, (p + 1) * LANES)
            out = (gates[:, 0 * width + p * LANES:0 * width + (p + 1) * LANES] * pair_tile(o_cmp, p)
                   + gates[:, 1 * width + p * LANES:1 * width + (p + 1) * LANES] * pair_tile(o_slc, p)
                   + gates[:, 2 * width + p * LANES:2 * width + (p + 1) * LANES] * pair_tile(o_win, p))
            o_ref[sl, cols] = out.astype(bf16)
        return carry

    lax.fori_loop(0, seq // qb, qblock, 0)


def _pad_w1(w1):
    h = w1.shape[1]
    w = w1.reshape(CMP_BLOCK, NSA_HEAD_DIM, h)
    return jnp.pad(w, ((0, 0), (0, LANES - NSA_HEAD_DIM), (0, 0))).reshape(CMP_BLOCK * LANES, h).astype(bf16)


def _nsa(main, small, cos_t, sin_t, cmp_pe_k, cmp_w1_k, cmp_w2_k, cmp_pe_v, cmp_w1_v, cmp_w2_v, batch, seq):
    t = main.shape[0]
    assert seq % (2 * KEY_CHUNK) == 0 and seq // SEL_BLOCK <= LANES
    dk = NSA_HEAD_DIM
    width = NSA_HPG * dk
    n_cmp = seq // CMP_STRIDE
    pad_l = ((0, 0), (0, LANES - dk))
    pek = jnp.pad(cmp_pe_k, pad_l)
    pev = jnp.pad(cmp_pe_v, pad_l)
    w1k = _pad_w1(cmp_w1_k)
    w1v = _pad_w1(cmp_w1_v)
    w2k = jnp.pad(cmp_w2_k, pad_l).astype(bf16)
    w2v = jnp.pad(cmp_w2_v, pad_l).astype(bf16)
    cstart = np.arange(LANES)[:, None] * CMP_STRIDE
    sstart = np.arange(LANES)[None, :] * SEL_BLOCK
    valid_c = np.arange(LANES)[:, None] < (seq - CMP_BLOCK) // CMP_STRIDE + 1
    valid_j = np.arange(LANES)[None, :] < seq // SEL_BLOCK
    ovl = ((cstart < sstart + SEL_BLOCK) & (cstart + CMP_BLOCK > sstart) & valid_c & valid_j).astype(np.float32)
    ovl = jnp.asarray(ovl[:n_cmp] if n_cmp <= LANES else ovl, bf16)
    eg = np.zeros((NSA_KV_GROUPS, LANES, 3 * width), np.float32)
    for gg in range(NSA_KV_GROUPS):
        for r in range(NSA_HPG):
            for j in range(3):
                eg[gg, SSM_HEADS + gg * NSA_HPG * 3 + r * 3 + j, j * width + r * dk:j * width + (r + 1) * dk] = 1.0
    eg = jnp.asarray(eg, bf16)

    kv_blk = C_KV // LANES
    kv_spec = lambda j: pl.BlockSpec((seq, LANES), lambda b, g: (b, kv_blk + j))
    const2 = lambda b, g: (0, 0)
    return pl.pallas_call(
        _nsa_kernel,
        grid=(batch, NSA_KV_GROUPS),
        in_specs=[pl.BlockSpec((seq, width), lambda b, g: (b, C_Q // width + g))]
                 + [kv_spec(j) for j in range(6)]
                 + [pl.BlockSpec((seq, LANES), lambda b, g: (b, 0))] * 3
                 + [pl.BlockSpec((CMP_BLOCK, LANES), const2, pipeline_mode=pl.Buffered(1)),
                    pl.BlockSpec((CMP_BLOCK * LANES, CMP_HIDDEN), const2, pipeline_mode=pl.Buffered(1)),
                    pl.BlockSpec((CMP_HIDDEN, LANES), const2, pipeline_mode=pl.Buffered(1))] * 2
                 + [pl.BlockSpec(ovl.shape, const2, pipeline_mode=pl.Buffered(1)),
                    pl.BlockSpec((1, LANES, 3 * width), lambda b, g: (g, 0, 0))],
        out_specs=pl.BlockSpec((seq, width), lambda b, g: (b, g)),
        out_shape=jax.ShapeDtypeStruct((t, NSA_WIDTH), bf16),
        scratch_shapes=[pltpu.VMEM((seq, LANES), bf16),
                        pltpu.VMEM((seq, LANES), bf16),
                        pltpu.VMEM((seq + WIN_PAD, LANES), bf16),
                        pltpu.VMEM((seq + WIN_PAD, LANES), bf16),
                        pltpu.VMEM((seq + CMP_BLOCK, LANES), f32),
                        pltpu.VMEM((seq + CMP_BLOCK, LANES), f32),
                        pltpu.VMEM((n_cmp, LANES), bf16),
                        pltpu.VMEM((n_cmp, LANES), bf16),
                        pltpu.VMEM((1, NSA_HPG * NSA_QTILE, LANES), bf16),
                        pltpu.VMEM((1, NSA_HPG * NSA_QTILE, LANES), f32),
                        pltpu.VMEM((1, seq // KEY_CHUNK, NSA_QTILE, KEY_CHUNK), f32),
                        pltpu.VMEM((seq // KEY_CHUNK, min(NSA_HPG * NSA_QTILE, SLC_ROWS), KEY_CHUNK), f32),
                        pltpu.VMEM((NSA_HPG * NSA_QTILE, LANES), f32),
                        pltpu.VMEM((2, NSA_HPG * NSA_QTILE, LANES), f32)],
        compiler_params=_cparams(("arbitrary", "arbitrary")),
    )(main, main, main, main, main, main, main, small, cos_t, sin_t,
      pek, w1k, w2k, pev, w1v, w2v, ovl, eg)


def _post_kernel(yn_ref, on_ref, gs_ref, ga_ref, x_ref, ada_ref, wso_ref, wno_ref, wo_ref, g2_ref, wr_ref, br_ref,
                 h1_ref, hn2_ref, cw_ref):
    tm = x_ref.shape[0]
    y_ssm = _dot(yn_ref[...], wso_ref[...])
    y_nsa = _dot(on_ref[...], wno_ref[...])
    merged = jax.nn.sigmoid(gs_ref[...].astype(f32)) * y_ssm + jax.nn.sigmoid(ga_ref[...].astype(f32)) * y_nsa
    ada = ada_ref[0]
    h1 = x_ref[...] + ada[2:3] * _dot(merged.astype(bf16), wo_ref[...])
    h1_ref[...] = h1
    hn2 = h1 * lax.rsqrt(jnp.mean(h1 * h1, axis=-1, keepdims=True) + NORM_EPS) * g2_ref[...]
    hn2 = hn2 * (1.0 + ada[4:5]) + ada[3:4]
    hn2_ref[...] = hn2.astype(bf16)

    logits = _dot_hp(hn2, wr_ref[...]) + br_ref[...]
    lane = lax.broadcasted_iota(jnp.int32, (tm, LANES), 1)
    is_g = lane < MOE_GROUPS
    lg = jnp.where(is_g, logits, NEG)
    eg = jnp.where(is_g, jnp.exp(lg - jnp.max(lg, axis=-1, keepdims=True)), 0.0)
    pg_all = eg / jnp.sum(eg, axis=-1, keepdims=True)
    pg = jnp.max(pg_all, axis=-1, keepdims=True)
    gsel = jnp.min(jnp.where(is_g & (pg_all == pg), lane, LANES), axis=-1, keepdims=True)
    member = (lane >= MOE_GROUPS) & (lane < MOE_GROUPS + N_EXPERTS) & (((lane - MOE_GROUPS) >> 3) == gsel)
    le = jnp.where(member, logits, NEG)
    ee = jnp.where(member, jnp.exp(le - jnp.max(le, axis=-1, keepdims=True)), 0.0)
    pe = jnp.where(member, ee / jnp.sum(ee, axis=-1, keepdims=True), -1.0)
    v0 = jnp.max(pe, axis=-1, keepdims=True)
    i0 = jnp.min(jnp.where(pe == v0, lane, LANES), axis=-1, keepdims=True)
    hit0 = lane == i0
    pe1 = jnp.where(hit0, -1.0, pe)
    v1 = jnp.max(pe1, axis=-1, keepdims=True)
    i1 = jnp.min(jnp.where(pe1 == v1, lane, LANES), axis=-1, keepdims=True)
    hit1 = lane == i1
    den = v0 + v1
    w0 = pg * v0 / den
    w1 = pg * v1 / den

    cw_ref[...] = jnp.where(hit0, w0, jnp.where(hit1, w1, 0.0))


def _post(yn, o_nsa, main, x2, ada3, w_ssm_out, w_nsa_out, w_o, norm2_g, w_route, b_route, seq):
    t, d = x2.shape
    tm = 512
    const = lambda i: (0, 0)
    return pl.pallas_call(
        _post_kernel,
        grid=(t // tm,),
        in_specs=[pl.BlockSpec((tm, SSM_D_INNER), lambda i: (i, 0)),
                  pl.BlockSpec((tm, NSA_WIDTH), lambda i: (i, 0)),
                  pl.BlockSpec((tm, d), lambda i: (i, C_GS // d)),
                  pl.BlockSpec((tm, d), lambda i: (i, C_GA // d)),
                  pl.BlockSpec((tm, d), lambda i: (i, 0)),
                  pl.BlockSpec((1, 6, d), lambda i: ((i * tm) // seq, 0, 0)),
                  pl.BlockSpec((SSM_D_INNER, d), const),
                  pl.BlockSpec((NSA_WIDTH, d), const),
                  pl.BlockSpec((d, d), const),
                  pl.BlockSpec((1, d), const),
                  pl.BlockSpec((d, LANES), const),
                  pl.BlockSpec((1, LANES), const)],
        out_specs=[pl.BlockSpec((tm, d), lambda i: (i, 0)),
                   pl.BlockSpec((tm, d), lambda i: (i, 0)),
                   pl.BlockSpec((tm, LANES), lambda i: (i, 0))],
        out_shape=[jax.ShapeDtypeStruct((t, d), f32),
                   jax.ShapeDtypeStruct((t, d), bf16),
                   jax.ShapeDtypeStruct((t, LANES), f32)],
        compiler_params=_cparams(("arbitrary",)),
    )(yn, o_nsa, main, main, x2, ada3, w_ssm_out, w_nsa_out, w_o, norm2_g.reshape(1, d), w_route, b_route)


def _moe_kernel(h1_ref, x_ref, cw_ref, ada_ref, wg_ref, wu_ref, wd_ref, fg_ref, o_ref, y_acc, hcat):
    g = pl.program_id(1)
    tm = x_ref.shape[0]
    hd = EXPERT_HIDDEN

    @pl.when(g == 0)
    def _():
        y_acc[...] = jnp.zeros(y_acc.shape, f32)

    x = x_ref[...]
    cw = cw_ref[...]
    lane = lax.broadcasted_iota(jnp.int32, (tm, LANES), 1)
    for e in range(EXPERTS_PER_GROUP):
        hg = _dot(x, wg_ref[0, e])
        hu = _dot(x, wu_ref[0, e])
        ce = jnp.sum(jnp.where(lane == MOE_GROUPS + g * EXPERTS_PER_GROUP + e, cw, 0.0), axis=-1, keepdims=True)
        hcat[:, e * hd:(e + 1) * hd] = (_silu(hg) * hu * ce).astype(bf16)
    y_acc[...] += _dot(hcat[...], wd_ref[0])

    @pl.when(g == MOE_GROUPS - 1)
    def _():
        h2 = h1_ref[...] + ada_ref[0][5:6] * y_acc[...]
        o_ref[...] = h2 * lax.rsqrt(jnp.mean(h2 * h2, axis=-1, keepdims=True) + NORM_EPS) * fg_ref[...]


def _moe(h1, hn2, cw, ada3, w_gate, w_up, w_down, final_g, seq):
    t, d = h1.shape
    tm = 512
    hd = EXPERT_HIDDEN
    epg = EXPERTS_PER_GROUP
    wg = w_gate.astype(bf16).reshape(MOE_GROUPS, epg, d, hd)
    wu = w_up.astype(bf16).reshape(MOE_GROUPS, epg, d, hd)
    wd = w_down.astype(bf16).reshape(MOE_GROUPS, epg * hd, d)
    return pl.pallas_call(
        _moe_kernel,
        grid=(t // tm, MOE_GROUPS),
        in_specs=[pl.BlockSpec((tm, d), lambda i, g: (i, 0)),
                  pl.BlockSpec((tm, d), lambda i, g: (i, 0)),
                  pl.BlockSpec((tm, LANES), lambda i, g: (i, 0)),
                  pl.BlockSpec((1, 6, d), lambda i, g: ((i * tm) // seq, 0, 0)),
                  pl.BlockSpec((1, epg, d, hd), lambda i, g: (g, 0, 0, 0)),
                  pl.BlockSpec((1, epg, d, hd), lambda i, g: (g, 0, 0, 0)),
                  pl.BlockSpec((1, epg * hd, d), lambda i, g: (g, 0, 0)),
                  pl.BlockSpec((1, d), lambda i, g: (0, 0))],
        out_specs=pl.BlockSpec((tm, d), lambda i, g: (i, 0)),
        out_shape=jax.ShapeDtypeStruct((t, d), f32),
        scratch_shapes=[pltpu.VMEM((tm, d), f32),
                        pltpu.VMEM((tm, epg * hd), bf16)],
        compiler_params=_cparams(("arbitrary", "arbitrary")),
    )(h1, hn2, cw, ada3, wg, wu, wd, final_g.reshape(1, d))


def _pack_w_in(w_in):
    o = np.cumsum([0, SSM_D_INNER, SSM_CONV_DIM, SSM_HEADS, NSA_WIDTH] + [NSA_KV_WIDTH] * 6
                  + [3 * NSA_HEADS, D_MODEL, D_MODEL])
    z_xbc = w_in[:, o[0]:o[2]]
    w_dt = w_in[:, o[2]:o[3]]
    w_q = w_in[:, o[3]:o[4]]
    w_kv = w_in[:, o[4]:o[10]]
    w_gn = w_in[:, o[10]:o[11]]
    w_gsa = w_in[:, o[11]:o[13]]
    w_main = jnp.concatenate([z_xbc, w_q, w_gsa, w_kv], axis=1).astype(bf16)
    w_small = jnp.concatenate(
        [w_dt, w_gn, jnp.zeros((w_in.shape[0], LANES - SSM_HEADS - 3 * NSA_HEADS), f32)], axis=1)
    return w_main, w_small


def _layer(h, ada3, cos_t, sin_t, batch, seq, norm1_g, w_in, conv_w, conv_b, dt_bias, a_log, d_skip, ssm_norm_g,
           w_ssm_out, cmp_pe_k, cmp_w1_k, cmp_w2_k, cmp_pe_v, cmp_w1_v, cmp_w2_v, w_nsa_out, w_o, norm2_g,
           w_router_group, b_router_group, w_router_expert, b_router_expert, w_exp_gate, w_exp_up, w_exp_down,
           final_g):
    t, d = h.shape
    w_main, w_small = _pack_w_in(w_in)
    main, small = _inproj(h, ada3, norm1_g, w_small, w_main, seq)
    yn = _ssd(main, small, conv_w, conv_b, dt_bias, a_log, d_skip, ssm_norm_g, batch, seq)
    o_nsa = _nsa(main, small, cos_t, sin_t, cmp_pe_k, cmp_w1_k, cmp_w2_k, cmp_pe_v, cmp_w1_v, cmp_w2_v, batch, seq)
    rpad = LANES - MOE_GROUPS - N_EXPERTS
    w_route = jnp.concatenate([w_router_group, w_router_expert, jnp.zeros((d, rpad), f32)], axis=1)
    b_route = jnp.concatenate([b_router_group, b_router_expert, jnp.zeros((rpad,), f32)]).reshape(1, LANES)
    h1, hn2, cw = _post(yn, o_nsa, main, h, ada3, w_ssm_out.astype(bf16), w_nsa_out.astype(bf16),
                        w_o.astype(bf16), norm2_g, w_route, b_route, seq)
    return _moe(h1, hn2, cw, ada3, w_exp_gate, w_exp_up, w_exp_down, final_g, seq)


def kernel(x, c, positions, w_ada, b_ada, norm1_g, w_in, conv_w, conv_b, dt_bias, a_log, d_skip, ssm_norm_g, w_ssm_out, cmp_pe_k, cmp_w1_k, cmp_w2_k, cmp_pe_v, cmp_w1_v, cmp_w2_v, w_nsa_out, w_o, norm2_g, w_router_group, b_router_group, w_router_expert, b_router_expert, w_exp_gate, w_exp_up, w_exp_down, final_g):
    batch, seq, d = x.shape
    depth = w_ada.shape[0]
    assert depth == 1, "final RMSNorm is fused into the single layer's MoE combine"
    cos_t, sin_t = _rope_tables(positions)
    h = x.reshape(batch * seq, d)
    l = 0
    ada3 = _ada(c, w_ada[l], b_ada[l]).reshape(batch, 6, d)
    out = _layer(h, ada3, cos_t, sin_t, batch, seq, norm1_g[l], w_in[l], conv_w[l], conv_b[l], dt_bias[l],
                 a_log[l], d_skip[l], ssm_norm_g[l], w_ssm_out[l], cmp_pe_k[l], cmp_w1_k[l], cmp_w2_k[l],
                 cmp_pe_v[l], cmp_w1_v[l], cmp_w2_v[l], w_nsa_out[l], w_o[l], norm2_g[l], w_router_group[l],
                 b_router_group[l], w_router_expert[l], b_router_expert[l], w_exp_gate[l], w_exp_up[l],
                 w_exp_down[l], final_g)
    return out.reshape(batch, seq, d)
```
